```python
import jax
import jax.numpy as jnp
from jax import lax
import numpy as np

D_MODEL = 1024
BATCH = 2
SEQ = 16384
DEPTH = 4
DEC_BATCH = 16
DEC_SEQ = 64
PAST_LEN = 2048

CHUNK = 64
N_MIXERS = 4
N_A = (DEPTH + 3) // N_MIXERS
N_B = (DEPTH + 2) // N_MIXERS
N_C = (DEPTH + 1) // N_MIXERS
N_D = DEPTH // N_MIXERS
EPS = 1e-6
HG_DK = 128
HG_HEADS = D_MODEL // HG_DK
HG_DV = D_MODEL // HG_HEADS
HG_KEY = HG_HEADS * HG_DK
RW_N = 64
RW_HEADS = D_MODEL // RW_N
RW_DECAY_LORA = 64
RW_A_LORA = 64
RW_GATE_LORA = 128
RW_DECAY_SCALE = float(np.exp(-0.5))
RW_GN_EPS = 64e-5
LRU_W = D_MODEL
LRU_BLOCKS = 8
LRU_BW = LRU_W // LRU_BLOCKS
LRU_CONV = 4
LRU_C = 8.0
CONF_K = 31
D_FF = ((8 * D_MODEL // 3 + 255) // 256) * 256

kernel_name = 'hybrid_streaming_encoder_step'


def _rmsnorm(x, g):
    xf = x.astype(jnp.float32)
    y = xf * lax.rsqrt(jnp.mean(xf * xf, axis=-1, keepdims=True) + EPS)
    return (y * g.astype(jnp.float32)).astype(x.dtype)


def _layernorm(x, g, b):
    xf = x.astype(jnp.float32)
    mu = jnp.mean(xf, axis=-1, keepdims=True)
    xc = xf - mu
    var = jnp.mean(xc * xc, axis=-1, keepdims=True)
    return (xc * lax.rsqrt(var + EPS) * g.astype(jnp.float32) + b.astype(jnp.float32)).astype(x.dtype)


def _causal_dwconv(x, buf, w, b):
    xp = jnp.concatenate([buf.astype(x.dtype), x], axis=1)
    y = lax.conv_general_dilated(xp, w[:, None, :].astype(x.dtype), window_strides=(1,),
                                 padding='VALID', dimension_numbers=('NWC', 'WIO', 'NWC'),
                                 feature_group_count=x.shape[-1])
    return y + b, xp[:, xp.shape[1] - (w.shape[0] - 1):]


def _gla_chunkwise(q, k, v, logf, s0):
    b_, seq_len, n_h, _ = q.shape
    n_blk = -(-seq_len // CHUNK)
    pad = n_blk * CHUNK - seq_len

    def blocks(t):
        t = jnp.pad(t, ((0, 0), (0, pad), (0, 0), (0, 0)))
        return t.reshape(b_, n_blk, CHUNK, n_h, t.shape[-1]).transpose(1, 0, 3, 2, 4)

    mask = jnp.tril(jnp.ones((CHUNK, CHUNK), bool))[:, :, None]

    def step(S, blk):
        qc, kc, vc, gc = blk
        cum = jnp.cumsum(gc, axis=2)
        rel = cum[:, :, :, None, :] - cum[:, :, None, :, :]
        dec = jnp.exp(jnp.where(mask, rel, -jnp.inf))
        scores = jnp.einsum('bhtd,bhtsd,bhsd->bhts', qc, dec, kc)
        o = (jnp.einsum('bhts,bhsv->bhtv', scores, vc)
             + jnp.einsum('bhtd,bhdv->bhtv', qc * jnp.exp(cum), S))
        last = cum[:, :, -1:, :]
        S = (jnp.exp(last[:, :, 0, :, None]) * S
             + jnp.einsum('bhsd,bhsv->bhdv', kc * jnp.exp(last - cum), vc))
        return S, o

    S, o = lax.scan(step, s0, (blocks(q), blocks(k), blocks(v), blocks(logf)))
    o = o.transpose(1, 0, 3, 2, 4).reshape(b_, n_blk * CHUNK, n_h, v.shape[-1])[:, :seq_len]
    return o, S


def _hgrn2(x, s0, lb, wq, wf, wi, wg, gn, wo):
    bsz, seq_len, _ = x.shape
    q = jax.nn.silu(x @ wq).astype(jnp.float32).reshape(bsz, seq_len, HG_HEADS, HG_DK) * (HG_DK ** -0.5)
    fl = (x @ wf).astype(jnp.float32)
    f = lb + (1.0 - lb) * jax.nn.sigmoid(fl)
    k = ((1.0 - lb) * jax.nn.sigmoid(-fl)).reshape(bsz, seq_len, HG_HEADS, HG_DK)
    logf = jnp.log(f).reshape(bsz, seq_len, HG_HEADS, HG_DK)
    v = (x @ wi).astype(jnp.float32).reshape(bsz, seq_len, HG_HEADS, HG_DV)
    o, S = _gla_chunkwise(q, k, v, logf, s0.astype(jnp.float32))
    o = o * lax.rsqrt(jnp.mean(o * o, axis=-1, keepdims=True) + EPS) * gn.astype(jnp.float32)
    o = o.reshape(bsz, seq_len, D_MODEL).astype(x.dtype) * jax.nn.silu(x @ wg)
    return o @ wo, S.astype(x.dtype)


def _rwkv7(x, s0, prev, mu, wr, wk, wv, w0, w1, w2, a0, a1, a2, g1, g2, k_k, k_a, r_k, ln_g, ln_b, wo):
    bsz, seq_len, _ = x.shape
    xprev = jnp.concatenate([prev[:, None].astype(x.dtype), x[:, :-1]], axis=1)
    xx = xprev - x
    xr, xw, xk, xv, xa, xg = [x + xx * mu[n] for n in range(6)]
    r = xr @ wr
    k = xk @ wk
    v = xv @ wv
    logw = -RW_DECAY_SCALE * jax.nn.sigmoid((w0 + jnp.tanh(xw @ w1) @ w2).astype(jnp.float32))
    a = jax.nn.sigmoid((a0 + (xa @ a1) @ a2).astype(jnp.float32))
    g = jax.nn.sigmoid(xg @ g1) @ g2

    def heads(t):
        return t.astype(jnp.float32).reshape(bsz, seq_len, RW_HEADS, RW_N)

    kk = heads(k * k_k)
    kk = kk / jnp.maximum(jnp.sqrt(jnp.sum(kk * kk, axis=-1, keepdims=True)), 1e-12)
    kh = heads(k.astype(jnp.float32) * (1.0 + (a - 1.0) * k_a))
    rh, vh, ah, wh = heads(r), heads(v), heads(a), jnp.exp(heads(logw))

    def step(S, inp):
        r_t, w_t, k_t, v_t, kk_t, a_t = inp
        sk = jnp.einsum('bhvk,bhk->bhv', S, kk_t)
        S = (S * w_t[:, :, None, :] - sk[..., None] * (kk_t * a_t)[:, :, None, :]
             + v_t[..., None] * k_t[:, :, None, :])
        return S, jnp.einsum('bhvk,bhk->bhv', S, r_t)

    seq = lambda t: jnp.moveaxis(t, 1, 0)
    S, o = lax.scan(step, s0.astype(jnp.float32), (seq(rh), seq(wh), seq(kh), seq(vh), seq(kk), seq(ah)))
    o = jnp.moveaxis(o, 0, 1)
    mean = jnp.mean(o, axis=-1, keepdims=True)
    oc = o - mean
    o = oc * lax.rsqrt(jnp.mean(oc * oc, axis=-1, keepdims=True) + RW_GN_EPS)
    o = o.reshape(bsz, seq_len, D_MODEL) * ln_g + ln_b
    bonus = jnp.sum(rh * kh * r_k, axis=-1, keepdims=True) * vh
    o = (o + bonus.reshape(bsz, seq_len, D_MODEL)).astype(x.dtype) * g
    return o @ wo, S.astype(x.dtype), x[:, -1]


def _lru_combine(left, right):
    return left[0] * right[0], right[0] * left[1] + right[1]


def _rglru(x, h0, buf, wy, wx, conv_w, conv_b, ga_w, ga_b, gx_w, gx_b, lam, wo):
    bsz, seq_len, _ = x.shape
    y = jax.nn.gelu(x @ wy)
    u, new_buf = _causal_dwconv(x @ wx, buf, conv_w, conv_b)
    ub = u.reshape(bsz, seq_len, LRU_BLOCKS, LRU_BW)

    def gate(w, b):
        z = jnp.einsum('blhi,hij->blhj', ub, w).reshape(bsz, seq_len, LRU_W) + b
        return jax.nn.sigmoid(z.astype(jnp.float32))

    r = gate(ga_w, ga_b)
    i = gate(gx_w, gx_b)
    log_a = -LRU_C * r * jax.nn.softplus(-lam.astype(jnp.float32))
    a = jnp.exp(log_a)
    bterm = jnp.sqrt(-jnp.expm1(2.0 * log_a)) * (i * u.astype(jnp.float32))
    bterm = bterm.at[:, 0].add(a[:, 0] * h0.astype(jnp.float32))
    _, h = lax.associative_scan(_lru_combine, (a, bterm), axis=1)
    out = (h.astype(x.dtype) * y) @ wo
    return out, h[:, -1].astype(x.dtype), new_buf


def _conformer_conv(x, buf, w1, b1, dw_w, dw_b, ln_g, ln_b, w2, b2):
    h = x @ w1 + b1
    u = h[..., :D_MODEL] * jax.nn.sigmoid(h[..., D_MODEL:])
    c, new_buf = _causal_dwconv(u, buf, dw_w, dw_b)
    c = jax.nn.silu(_layernorm(c, ln_g, ln_b))
    return c @ w2 + b2, new_buf


def _swiglu(x, w1, w3, w2):
    return (jax.nn.silu(x @ w1) * (x @ w3)) @ w2


def setup_inputs(seed: int = 0) -> dict:
    key = jax.random.key(seed)
    ks = iter(jax.random.split(key, 96))
    f32 = jnp.float32
    D = D_MODEL

    def nrm(shape, scale=1.0):
        return scale * jax.random.normal(next(ks), shape, f32)

    def unif(shape, lo, hi):
        return jax.random.uniform(next(ks), shape, f32, lo, hi)

    def gain(shape):
        return 1.0 + nrm(shape, 0.05)

    a_base = unif((N_C, LRU_W), 0.9, 0.999) ** (1.0 / LRU_C)
    return {
        'x_prompt': nrm((BATCH, SEQ, D)),
        'x_sample': nrm((DEC_BATCH, DEC_SEQ, D)),
        'state_hgrn': nrm((N_A, DEC_BATCH, HG_HEADS, HG_DK, HG_DV), 0.5),
        'state_rwkv': nrm((N_B, DEC_BATCH, RW_HEADS, RW_N, RW_N), 0.3),
        'state_rwkv_shift': nrm((N_B, DEC_BATCH, D)),
        'state_lru': nrm((N_C, DEC_BATCH, LRU_W), 0.5),
        'state_lru_conv': nrm((N_C, DEC_BATCH, LRU_CONV - 1, LRU_W)),
        'state_conf_conv': nrm((N_D, DEC_BATCH, CONF_K - 1, D), 0.5),
        'norm_mix': gain((DEPTH, D)),
        'norm_ffn': gain((DEPTH, D)),
        'norm_final': gain((D,)),
        'hg_wq': nrm((N_A, D, HG_KEY), D ** -0.5),
        'hg_wf': nrm((N_A, D, HG_KEY), D ** -0.5),
        'hg_wi': nrm((N_A, D, D), D ** -0.5),
        'hg_wg': nrm((N_A, D, D), D ** -0.5),
        'hg_gn': gain((N_A, HG_DV)),
        'hg_wo': nrm((N_A, D, D), D ** -0.5),
        'hg_lb': nrm((N_A + 1, HG_KEY)),
        'rw_mu': unif((N_B, 6, D), 0.0, 1.0),
        'rw_wr': nrm((N_B, D, D), D ** -0.5),
        'rw_wk': nrm((N_B, D, D), D ** -0.5),
        'rw_wv': nrm((N_B, D, D), D ** -0.5),
        'rw_w0': unif((N_B, D), -5.0, 1.0),
        'rw_w1': nrm((N_B, D, RW_DECAY_LORA), D ** -0.5),
        'rw_w2': nrm((N_B, RW_DECAY_LORA, D), 0.5 * RW_DECAY_LORA ** -0.5),
        'rw_a0': nrm((N_B, D), 0.5),
        'rw_a1': nrm((N_B, D, RW_A_LORA), D ** -0.5),
        'rw_a2': nrm((N_B, RW_A_LORA, D), 0.5 * RW_A_LORA ** -0.5),
        'rw_g1': nrm((N_B, D, RW_GATE_LORA), D ** -0.5),
        'rw_g2': nrm((N_B, RW_GATE_LORA, D), RW_GATE_LORA ** -0.5),
        'rw_kk': 0.85 + nrm((N_B, D), 0.05),
        'rw_ka': gain((N_B, D)),
        'rw_rk': nrm((N_B, RW_HEADS, RW_N), 0.1),
        'rw_ln_g': gain((N_B, D)),
        'rw_ln_b': nrm((N_B, D), 0.01),
        'rw_wo': nrm((N_B, D, D), D ** -0.5),
        'lru_wy': nrm((N_C, D, LRU_W), D ** -0.5),
        'lru_wx': nrm((N_C, D, LRU_W), D ** -0.5),
        'lru_conv_w': nrm((N_C, LRU_CONV, LRU_W), LRU_CONV ** -0.5),
        'lru_conv_b': nrm((N_C, LRU_W), 0.01),
        'lru_ga_w': nrm((N_C, LRU_BLOCKS, LRU_BW, LRU_BW), LRU_BW ** -0.5),
        'lru_ga_b': nrm((N_C, LRU_W), 0.01),
        'lru_gx_w': nrm((N_C, LRU_BLOCKS, LRU_BW, LRU_BW), LRU_BW ** -0.5),
        'lru_gx_b': nrm((N_C, LRU_W), 0.01),
        'lru_lam': jnp.log(a_base) - jnp.log1p(-a_base),
        'lru_wo': nrm((N_C, LRU_W, D), LRU_W ** -0.5),
        'cf_w1': nrm((N_D, D, 2 * D), D ** -0.5),
        'cf_b1': nrm((N_D, 2 * D), 0.01),
        'cf_dw_w': nrm((N_D, CONF_K, D), CONF_K ** -0.5),
        'cf_dw_b': nrm((N_D, D), 0.01),
        'cf_ln_g': gain((N_D, D)),
        'cf_ln_b': nrm((N_D, D), 0.01),
        'cf_w2': nrm((N_D, D, D), D ** -0.5),
        'cf_b2': nrm((N_D, D), 0.01),
        'ffn_w1': nrm((DEPTH, D, D_FF), D ** -0.5),
        'ffn_w3': nrm((DEPTH, D, D_FF), D ** -0.5),
        'ffn_w2': nrm((DEPTH, D_FF, D), D_FF ** -0.5),
    }


def reference(x_prompt, x_sample, state_hgrn, state_rwkv, state_rwkv_shift, state_lru, state_lru_conv,
              state_conf_conv, norm_mix, norm_ffn, norm_final, hg_wq, hg_wf, hg_wi, hg_wg, hg_gn, hg_wo,
              hg_lb, rw_mu, rw_wr, rw_wk, rw_wv, rw_w0, rw_w1, rw_w2, rw_a0, rw_a1, rw_a2, rw_g1, rw_g2,
              rw_kk, rw_ka, rw_rk, rw_ln_g, rw_ln_b, rw_wo, lru_wy, lru_wx, lru_conv_w, lru_conv_b,
              lru_ga_w, lru_ga_b, lru_gx_w, lru_gx_b, lru_lam, lru_wo, cf_w1, cf_b1, cf_dw_w, cf_dw_b,
              cf_ln_g, cf_ln_b, cf_w2, cf_b2, ffn_w1, ffn_w3, ffn_w2):
    lb_all = jnp.cumsum(jax.nn.softmax(hg_lb.astype(jnp.float32), axis=0), axis=0)

    def stack(x, s_hg, s_rw, s_sh, s_lh, s_lc, s_cf):
        o_hg, o_rw, o_sh, o_lh, o_lc, o_cf = [], [], [], [], [], []
        for i in range(DEPTH):
            m, j = i % N_MIXERS, i // N_MIXERS
            h = _rmsnorm(x, norm_mix[i])
            if m == 0:
                out, s = _hgrn2(h, s_hg[j], lb_all[j], hg_wq[j], hg_wf[j], hg_wi[j], hg_wg[j], hg_gn[j], hg_wo[j])
                o_hg.append(s)
            elif m == 1:
                out, s, sh = _rwkv7(h, s_rw[j], s_sh[j], rw_mu[j], rw_wr[j], rw_wk[j], rw_wv[j], rw_w0[j],
                                    rw_w1[j], rw_w2[j], rw_a0[j], rw_a1[j], rw_a2[j], rw_g1[j], rw_g2[j],
                                    rw_kk[j], rw_ka[j], rw_rk[j], rw_ln_g[j], rw_ln_b[j], rw_wo[j])
                o_rw.append(s)
                o_sh.append(sh)
            elif m == 2:
                out, hl, cb = _rglru(h, s_lh[j], s_lc[j], lru_wy[j], lru_wx[j], lru_conv_w[j], lru_conv_b[j],
                                     lru_ga_w[j], lru_ga_b[j], lru_gx_w[j], lru_gx_b[j], lru_lam[j], lru_wo[j])
                o_lh.append(hl)
                o_lc.append(cb)
            else:
                out, cb = _conformer_conv(h, s_cf[j], cf_w1[j], cf_b1[j], cf_dw_w[j], cf_dw_b[j],
                                          cf_ln_g[j], cf_ln_b[j], cf_w2[j], cf_b2[j])
                o_cf.append(cb)
            x = x + out
            x = x + _swiglu(_rmsnorm(x, norm_ffn[i]), ffn_w1[i], ffn_w3[i], ffn_w2[i])
        return (_rmsnorm(x, norm_final), jnp.stack(o_hg), jnp.stack(o_rw), jnp.stack(o_sh),
                jnp.stack(o_lh), jnp.stack(o_lc), jnp.stack(o_cf))

    bp, dt = x_prompt.shape[0], x_prompt.dtype
    zeros_like_state = lambda s: jnp.zeros((s.shape[0], bp) + s.shape[2:], dt)
    y_prompt, p_hg, p_rw, p_sh, p_lh, p_lc, p_cf = stack(
        x_prompt, zeros_like_state(state_hgrn), zeros_like_state(state_rwkv),
        zeros_like_state(state_rwkv_shift), zeros_like_state(state_lru),
        zeros_like_state(state_lru_conv), zeros_like_state(state_conf_conv))
    y_sample, s_hg, s_rw, s_sh, s_lh, s_lc, s_cf = stack(
        x_sample, state_hgrn, state_rwkv, state_rwkv_shift, state_lru, state_lru_conv, state_conf_conv)
    return (y_prompt, y_sample, p_hg, p_rw, p_sh, p_lh, p_lc, p_cf, s_hg, s_rw, s_sh, s_lh, s_lc, s_cf)
```

```python
import functools
from typing import NamedTuple

import numpy as np
import jax
import jax.numpy as jnp
from jax import lax
from jax.experimental import pallas as pl
from jax.experimental.pallas import tpu as pltpu

F32 = jnp.float32
BF16 = jnp.bfloat16

EPS = 1e-6
CHUNK = 64
SUB = 16
LANES = 128
SUBLANES = 8
HG_DK = 128
RW_N = 64
RW_DECAY_SCALE = 0.6065306597126334
RW_GN_EPS = 64e-5
LRU_BLOCKS = 8
LRU_C = 8.0
VMEM_LIMIT = 56 * 1024 * 1024


class Geom(NamedTuple):
    n_chunks: int
    npc: int
    cps_p: int
    cps_s: int
    nb_p: int
    n_seq: int


def _chunk_pos(c, g: Geom):
    in_p = c < g.npc
    cs = c - g.npc
    seq = jnp.where(in_p, c // g.cps_p, g.nb_p + cs // g.cps_s)
    first = jnp.where(in_p, c % g.cps_p == 0, cs % g.cps_s == 0)
    last = jnp.where(in_p, c % g.cps_p == g.cps_p - 1, cs % g.cps_s == g.cps_s - 1)
    return seq, first, last


def _seq_index_map(g: Geom, ndim):
    def index_map(c):
        seq, _, _ = _chunk_pos(c, g)
        return (seq,) + (0,) * (ndim - 1)
    return index_map


def _dot(a, b):
    return jnp.dot(a.astype(BF16), b.astype(BF16), preferred_element_type=F32)


def _dot_nt(a, b):
    return lax.dot_general(a.astype(BF16), b.astype(BF16), (((1,), (1,)), ((), ())),
                           preferred_element_type=F32)


def _dot_tn(a, b):
    return lax.dot_general(a.astype(BF16), b.astype(BF16), (((0,), (0,)), ((), ())),
                           preferred_element_type=F32)


def _split3(x):
    hi = x.astype(BF16)
    r1 = x - hi.astype(F32)
    mid = r1.astype(BF16)
    lo = (r1 - mid.astype(F32)).astype(BF16)
    return hi, mid, lo


def _dot_sel(sel_bf16, x):
    hi, mid, lo = _split3(x)
    d = lambda t: jnp.dot(sel_bf16, t, preferred_element_type=F32)
    return d(hi) + d(mid) + d(lo)


def _dot_sel_r(x, sel_bf16):
    hi, mid, lo = _split3(x)
    d = lambda t: jnp.dot(t, sel_bf16, preferred_element_type=F32)
    return d(hi) + d(mid) + d(lo)


def _rms(x, g):
    return x * lax.rsqrt(jnp.mean(x * x, axis=-1, keepdims=True) + EPS) * g


def _sigmoid(x):
    return 1.0 / (1.0 + jnp.exp(-x))


def _silu(x):
    return x * _sigmoid(x)


def _tri_incl(n):
    r = lax.broadcasted_iota(jnp.int32, (n, n), 0)
    c = lax.broadcasted_iota(jnp.int32, (n, n), 1)
    return (r >= c).astype(BF16)


def _const_spec(shape):
    return pl.BlockSpec(shape, lambda *_: (0,) * len(shape))


def _params(sem="arbitrary"):
    return pltpu.CompilerParams(dimension_semantics=(sem,), vmem_limit_bytes=VMEM_LIMIT)


def _post_ffn_kernel(x_ref, y_ref, wo_ref, bo_ref, gn_ref, w1_ref, w3_ref, w2_ref, gf_ref, o_ref,
                     *, n_ff_chunks, final):
    x1 = x_ref[...] + _dot(y_ref[...], wo_ref[...]) + bo_ref[...]
    h = _rms(x1, gn_ref[...]).astype(BF16)

    def body(c, acc):
        a = jnp.dot(h, w1_ref[c], preferred_element_type=F32)
        b = jnp.dot(h, w3_ref[c], preferred_element_type=F32)
        return acc + _dot(_silu(a) * b, w2_ref[c])

    out = x1 + lax.fori_loop(0, n_ff_chunks, body, jnp.zeros_like(x1))
    if final:
        out = _rms(out, gf_ref[...])
    o_ref[...] = out


def _post_ffn(x, y, wo, bo, gn, w1, w3, w2, gf, *, final, tm=512, ff_chunk=256):
    t, d = x.shape
    dff = w1.shape[1]
    nc = dff // ff_chunk
    assert nc * ff_chunk == dff and t % tm == 0
    w1c = w1.astype(BF16).reshape(d, nc, ff_chunk).transpose(1, 0, 2)
    w3c = w3.astype(BF16).reshape(d, nc, ff_chunk).transpose(1, 0, 2)
    w2c = w2.astype(BF16).reshape(nc, ff_chunk, d)
    row = pl.BlockSpec((tm, d), lambda i: (i, 0))
    return pl.pallas_call(
        functools.partial(_post_ffn_kernel, n_ff_chunks=nc, final=final),
        grid=(t // tm,),
        in_specs=[row, pl.BlockSpec((tm, y.shape[1]), lambda i: (i, 0)),
                  _const_spec(wo.shape), _const_spec((1, d)), _const_spec((1, d)),
                  _const_spec(w1c.shape), _const_spec(w3c.shape), _const_spec(w2c.shape),
                  _const_spec((1, d))],
        out_specs=row,
        out_shape=jax.ShapeDtypeStruct((t, d), F32),
        compiler_params=_params("parallel"),
        name="post_ffn",
    )(x, y, wo.astype(BF16), bo.reshape(1, d), gn.reshape(1, d), w1c, w3c, w2c, gf.reshape(1, d))


def _hgrn_proj_kernel(x_ref, gn_ref, lbp_ref, wq_ref, wf_ref, wi_ref, wg_ref,
                      q_ref, k_ref, lf_ref, v_ref, gate_ref, *, layer):
    h = _rms(x_ref[...], gn_ref[...]).astype(BF16)
    lbp = lbp_ref[...]
    e = jnp.exp(lbp - jnp.max(lbp, axis=0, keepdims=True))
    lb = jnp.sum(e[:layer + 1], axis=0, keepdims=True) / jnp.sum(e, axis=0, keepdims=True)
    q_ref[...] = _silu(jnp.dot(h, wq_ref[...], preferred_element_type=F32)) * (HG_DK ** -0.5)
    fl = jnp.dot(h, wf_ref[...], preferred_element_type=F32)
    k_ref[...] = (1.0 - lb) * _sigmoid(-fl)
    lf_ref[...] = jnp.log(lb + (1.0 - lb) * _sigmoid(fl))
    v_ref[...] = jnp.dot(h, wi_ref[...], preferred_element_type=F32)
    gate_ref[...] = _silu(jnp.dot(h, wg_ref[...], preferred_element_type=F32))


def _hgrn_proj(x, gn, lbp, wq, wf, wi, wg, *, layer, tm=256):
    t, d = x.shape
    row = pl.BlockSpec((tm, d), lambda i: (i, 0))
    wspec = _const_spec((d, d))
    out = jax.ShapeDtypeStruct((t, d), F32)
    return pl.pallas_call(
        functools.partial(_hgrn_proj_kernel, layer=layer),
        grid=(t // tm,),
        in_specs=[row, _const_spec((1, d)), _const_spec(lbp.shape), wspec, wspec, wspec, wspec],
        out_specs=[row] * 5,
        out_shape=[out] * 5,
        compiler_params=_params("parallel"),
        name="hgrn_proj",
    )(x, gn.reshape(1, d), lbp, wq.astype(BF16), wf.astype(BF16), wi.astype(BF16), wg.astype(BF16))


def _gla_kernel(q_ref, k_ref, lf_ref, v_ref, gate_ref, gn_ref, si_ref, og_ref, so_ref,
                st_ref, kp_ref, cp_ref, vp_ref, *, geom, n_heads):
    c = pl.program_id(0)
    _, first, last = _chunk_pos(c, geom)

    @pl.when(first)
    def _():
        for h in range(n_heads):
            st_ref[h] = si_ref[0, h].T

    cum = _dot_sel(_tri_incl(CHUNK), lf_ref[...])
    zpad = jnp.zeros((SUB, kp_ref.shape[1]), F32)
    kp_ref[:SUB] = zpad
    cp_ref[:SUB] = zpad
    vp_ref[:SUB] = zpad
    kp_ref[SUB:] = k_ref[...]
    cp_ref[SUB:] = cum
    vp_ref[SUB:] = v_ref[...]
    ones = jnp.ones((HG_DK, LANES), BF16)
    row_in_sub = lax.broadcasted_iota(jnp.int32, (CHUNK, HG_DK), 0) % SUB
    n_sub = CHUNK // SUB

    for h in range(n_heads):
        sl = slice(h * HG_DK, (h + 1) * HG_DK)
        q = q_ref[:, sl]
        k = k_ref[:, sl]
        v = v_ref[:, sl]
        cm = cum[:, sl]
        st = st_ref[h]
        cl = cm[CHUNK - 1:CHUNK]
        o = _dot_nt(q * jnp.exp(cm), st)
        blocks = [jnp.zeros((SUB, HG_DK), F32)]
        for i in range(1, n_sub):
            lo, hi = i * SUB, (i + 1) * SUB
            ref = cm[lo - 1:lo]
            qi = q[lo:hi] * jnp.exp(cm[lo:hi] - ref)
            kj = k[:lo] * jnp.exp(ref - cm[:lo])
            blocks.append(_dot(_dot_nt(qi, kj), v[:lo]))
        o = o + jnp.concatenate(blocks, axis=0)
        for delta in range(SUB):
            off = SUB - delta
            ks = kp_ref[off:off + CHUNK, sl]
            cs = cp_ref[off:off + CHUNK, sl]
            vs = vp_ref[off:off + CHUNK, sl]
            e = jnp.where(row_in_sub >= delta, jnp.exp(cm - cs), 0.0) * (q * ks)
            o = o + jnp.dot(e.astype(BF16), ones, preferred_element_type=F32) * vs
        st_new = st * jnp.exp(cl) + _dot_tn(v, k * jnp.exp(cl - cm))
        st_ref[h] = st_new
        o = o * lax.rsqrt(jnp.mean(o * o, axis=-1, keepdims=True) + EPS) * gn_ref[...]
        og_ref[:, sl] = o * gate_ref[:, sl]

        @pl.when(last)
        def _():
            so_ref[0, h] = st_new.T


def _gla(q, k, lf, v, gate, gn, s_in, geom: Geom):
    t, d = q.shape
    n_heads = d // HG_DK
    row = pl.BlockSpec((CHUNK, d), lambda c: (c, 0))
    sspec = pl.BlockSpec((1,) + s_in.shape[1:], _seq_index_map(geom, s_in.ndim))
    pad = pltpu.VMEM((CHUNK + SUB, d), F32)
    return pl.pallas_call(
        functools.partial(_gla_kernel, geom=geom, n_heads=n_heads),
        grid=(geom.n_chunks,),
        in_specs=[row] * 5 + [_const_spec((1, HG_DK)), sspec],
        out_specs=[row, sspec],
        out_shape=[jax.ShapeDtypeStruct((t, d), F32), jax.ShapeDtypeStruct(s_in.shape, F32)],
        scratch_shapes=[pltpu.VMEM((n_heads, HG_DK, HG_DK), F32), pad, pad, pad],
        compiler_params=_params(),
        name="hgrn_gla",
    )(q, k, lf, v, gate, gn.reshape(1, HG_DK), s_in)


def _hgrn_layer(x, gn, s_in, p, j, geom):
    q, k, lf, v, gate = _hgrn_proj(x, gn, p["hg_lb"], p["hg_wq"][j], p["hg_wf"][j],
                                   p["hg_wi"][j], p["hg_wg"][j], layer=j)
    return _gla(q, k, lf, v, gate, p["hg_gn"][j], s_in, geom)


def _rwkv_proj_kernel(x_ref, xp_ref, st_ref, gn_ref, mu_ref, wr_ref, wk_ref, wv_ref, w0_ref, w1_ref,
                      w2_ref, a0_ref, a1_ref, a2_ref, g1_ref, g2_ref, kk_ref, ka_ref,
                      r_ref, lw_ref, kh_ref, v_ref, kkr_ref, a_ref, g_ref, hl_ref,
                      win_ref, *, geom, tm):
    i = pl.program_id(0)
    cpt = tm // CHUNK
    gn = gn_ref[...]
    h = _rms(x_ref[...], gn)
    win_ref[SUBLANES:] = h
    win_ref[:SUBLANES] = _rms(xp_ref[...], gn)
    for j in range(cpt):
        last_row = SUBLANES + (j + 1) * CHUNK - 1
        hl_ref[j] = win_ref[last_row:last_row + 1]
    for j in range(cpt):
        _, first, _ = _chunk_pos(i * cpt + j, geom)

        @pl.when(first)
        def _():
            row = SUBLANES - 1 + j * CHUNK
            win_ref[row:row + 1] = st_ref[j]

    xx = win_ref[SUBLANES - 1:SUBLANES - 1 + tm] - h
    mix = lambda n: (h + xx * mu_ref[n:n + 1]).astype(BF16)
    dot = lambda a, w_ref: jnp.dot(a, w_ref[...], preferred_element_type=F32)
    r_ref[...] = dot(mix(0), wr_ref)
    k = dot(mix(2), wk_ref)
    v_ref[...] = dot(mix(3), wv_ref)
    tw = jnp.tanh(dot(mix(1), w1_ref)).astype(BF16)
    lw_ref[...] = -RW_DECAY_SCALE * _sigmoid(w0_ref[...] + dot(tw, w2_ref))
    a = _sigmoid(a0_ref[...] + dot(dot(mix(4), a1_ref).astype(BF16), a2_ref))
    g_ref[...] = dot(_sigmoid(dot(mix(5), g1_ref)).astype(BF16), g2_ref)
    a_ref[...] = a
    kkr_ref[...] = k * kk_ref[...]
    kh_ref[...] = k * (1.0 + (a - 1.0) * ka_ref[...])


def _rwkv_proj(x, gn, start_rows, p, j, geom, *, tm=256):
    t, d = x.shape
    cpt = tm // CHUNK
    row = pl.BlockSpec((tm, d), lambda i: (i, 0))
    prev = pl.BlockSpec((SUBLANES, d), lambda i: (jnp.maximum(i * (tm // SUBLANES) - 1, 0), 0))
    vec = _const_spec((1, d))
    bf = lambda w: w.astype(BF16)
    ws = [bf(p["rw_wr"][j]), bf(p["rw_wk"][j]), bf(p["rw_wv"][j]), p["rw_w0"][j].reshape(1, d),
          bf(p["rw_w1"][j]), bf(p["rw_w2"][j]), p["rw_a0"][j].reshape(1, d), bf(p["rw_a1"][j]),
          bf(p["rw_a2"][j]), bf(p["rw_g1"][j]), bf(p["rw_g2"][j]), p["rw_kk"][j].reshape(1, d),
          p["rw_ka"][j].reshape(1, d)]
    out = jax.ShapeDtypeStruct((t, d), F32)
    return pl.pallas_call(
        functools.partial(_rwkv_proj_kernel, geom=geom, tm=tm),
        grid=(t // tm,),
        in_specs=[row, prev, pl.BlockSpec((cpt, 1, d), lambda i: (i, 0, 0)), vec,
                  _const_spec((6, d))] + [_const_spec(w.shape) for w in ws],
        out_specs=[row] * 7 + [pl.BlockSpec((cpt, 1, d), lambda i: (i, 0, 0))],
        out_shape=[out] * 7 + [jax.ShapeDtypeStruct((geom.n_chunks, 1, d), F32)],
        scratch_shapes=[pltpu.VMEM((tm + SUBLANES, d), F32)],
        compiler_params=_params(),
        name="rwkv_proj",
    )(x, x, start_rows.reshape(geom.n_chunks, 1, d), gn.reshape(1, d), p["rw_mu"][j], *ws)


def _block_diag(x, bd_mask):
    return jnp.where(bd_mask, jnp.concatenate([x, x], axis=0), 0.0)


def _rwkv_kernel(r_ref, lw_ref, kh_ref, v_ref, kkr_ref, a_ref, g_ref, rk_ref, lg_ref, lb_ref, si_ref,
                 y_ref, so_ref, st_ref, *, geom, n_pairs):
    c = pl.program_id(0)
    _, first, last = _chunk_pos(c, geom)
    n = RW_N
    p2 = 2 * n

    @pl.when(first)
    def _():
        z = jnp.zeros((n, n), F32)
        for p in range(n_pairs):
            sa, sb = si_ref[0, 2 * p], si_ref[0, 2 * p + 1]
            st_ref[p] = jnp.concatenate([jnp.concatenate([sa, z], axis=1),
                                         jnp.concatenate([z, sb], axis=1)], axis=0)

    tri = _tri_incl(CHUNK)
    lw = lw_ref[...]
    cw = _dot_sel(tri, lw)
    w_inc = jnp.exp(cw)
    w_exc = jnp.exp(cw - lw)
    w_inv = jnp.exp(-cw)
    w_end = jnp.exp(cw[CHUNK - 1:CHUNK] - cw)

    ri = lax.broadcasted_iota(jnp.int32, (p2, p2), 0)
    ci = lax.broadcasted_iota(jnp.int32, (p2, p2), 1)
    bd_mask = (ri // n) == (ci // n)
    ones_bd = bd_mask.astype(BF16)
    t_idx = lax.broadcasted_iota(jnp.int32, (CHUNK, p2), 0)
    s_idx = lax.broadcasted_iota(jnp.int32, (CHUNK, p2), 1) % n
    strict = s_idx < t_idx
    incl = s_idx <= t_idx
    eye = (s_idx == t_idx).astype(F32)
    bd = lambda x: _block_diag(x, bd_mask)

    for p in range(n_pairs):
        sl = slice(p * p2, (p + 1) * p2)
        r, kh, v, a, g = r_ref[:, sl], kh_ref[:, sl], v_ref[:, sl], a_ref[:, sl], g_ref[:, sl]
        kkr = kkr_ref[:, sl]
        kk = kkr / jnp.maximum(jnp.sqrt(_dot_sel_r(kkr * kkr, ones_bd)), 1e-12)
        b = kk * a
        wi, wx, wv_, we = w_inc[:, sl], w_exc[:, sl], w_inv[:, sl], w_end[:, sl]
        st = st_ref[p]
        kk_t, r_t = kk * wx, r * wi
        b_t, k_t = b * wv_, kh * wv_
        bd_b, bd_k, bd_v = bd(b_t), bd(k_t), bd(v)
        a_ab = jnp.where(strict, _dot_nt(kk_t, bd_b), 0.0)
        a_ak = jnp.where(strict, _dot_nt(kk_t, bd_k), 0.0)
        b_rb = jnp.where(incl, _dot_nt(r_t, bd_b), 0.0)
        b_rk = jnp.where(incl, _dot_nt(r_t, bd_k), 0.0)
        npow = -a_ab
        tinv = eye + npow
        for _ in range(5):
            npow = _dot(npow, bd(npow))
            tinv = _dot(tinv, bd(eye + npow))
        rhs = _dot_nt(kk_t, st) + _dot(a_ak, bd_v)
        u = _dot(tinv, bd(rhs))
        o = _dot_nt(r_t, st) - _dot(b_rb, bd(u)) + _dot(b_rk, bd_v)
        upd = _dot_tn(jnp.concatenate([v, -u], axis=0), jnp.concatenate([kh * we, b * we], axis=0))
        st_new = st * wi[CHUNK - 1:CHUNK] + jnp.where(bd_mask, upd, 0.0)
        st_ref[p] = st_new
        mean = _dot_sel_r(o, ones_bd) * (1.0 / n)
        oc = o - mean
        var = _dot_sel_r(oc * oc, ones_bd) * (1.0 / n)
        on = oc * lax.rsqrt(var + RW_GN_EPS) * lg_ref[:, sl] + lb_ref[:, sl]
        bonus = _dot_sel_r(r * kh * rk_ref[:, sl], ones_bd) * v
        y_ref[:, sl] = (on + bonus) * g

        @pl.when(last)
        def _():
            so_ref[0, 2 * p] = st_new[:n, :n]
            so_ref[0, 2 * p + 1] = st_new[n:, n:]


def _rwkv(r, lw, kh, v, kkr, a, g, rk, lg, lb, s_in, geom: Geom):
    t, d = r.shape
    n_pairs = d // (2 * RW_N)
    row = pl.BlockSpec((CHUNK, d), lambda c: (c, 0))
    vec = _const_spec((1, d))
    sspec = pl.BlockSpec((1,) + s_in.shape[1:], _seq_index_map(geom, s_in.ndim))
    return pl.pallas_call(
        functools.partial(_rwkv_kernel, geom=geom, n_pairs=n_pairs),
        grid=(geom.n_chunks,),
        in_specs=[row] * 7 + [vec] * 3 + [sspec],
        out_specs=[row, sspec],
        out_shape=[jax.ShapeDtypeStruct((t, d), F32), jax.ShapeDtypeStruct(s_in.shape, F32)],
        scratch_shapes=[pltpu.VMEM((n_pairs, 2 * RW_N, 2 * RW_N), F32)],
        compiler_params=_params(),
        name="rwkv_scan",
    )(r, lw, kh, v, kkr, a, g, rk.reshape(1, d), lg.reshape(1, d), lb.reshape(1, d), s_in)


def _rwkv_layer(x, gn, s_in, shift_in, p, j, geom):
    d = x.shape[1]
    seq_first = _first_chunks(geom)
    start_rows = jnp.zeros((geom.n_chunks, d), F32).at[seq_first].set(shift_in)
    r, lw, kh, v, kkr, a, g, hl = _rwkv_proj(x, gn, start_rows, p, j, geom)
    y, s_out = _rwkv(r, lw, kh, v, kkr, a, g, p["rw_rk"][j], p["rw_ln_g"][j], p["rw_ln_b"][j], s_in, geom)
    return y, s_out, hl.reshape(geom.n_chunks, d)[_last_chunks(geom)]


def _first_chunks(g: Geom):
    return np.concatenate([np.arange(g.nb_p) * g.cps_p,
                           g.npc + np.arange(g.n_seq - g.nb_p) * g.cps_s])


def _last_chunks(g: Geom):
    return _first_chunks(g) + np.concatenate([np.full(g.nb_p, g.cps_p - 1),
                                              np.full(g.n_seq - g.nb_p, g.cps_s - 1)])


def _gelu_tanh(x):
    return 0.5 * x * (1.0 + jnp.tanh(0.7978845608028654 * (x + 0.044715 * x * x * x)))


def _lru_proj_kernel(x_ref, gn_ref, wy_ref, wx_ref, y_ref, ux_ref):
    h = _rms(x_ref[...], gn_ref[...]).astype(BF16)
    y_ref[...] = _gelu_tanh(jnp.dot(h, wy_ref[...], preferred_element_type=F32))
    ux_ref[...] = jnp.dot(h, wx_ref[...], preferred_element_type=F32)


def _lru_proj(x, gn, wy, wx, *, tm=512):
    t, d = x.shape
    row = pl.BlockSpec((tm, d), lambda i: (i, 0))
    out = jax.ShapeDtypeStruct((t, wy.shape[1]), F32)
    return pl.pallas_call(
        _lru_proj_kernel,
        grid=(t // tm,),
        in_specs=[row, _const_spec((1, d)), _const_spec(wy.shape), _const_spec(wx.shape)],
        out_specs=[pl.BlockSpec((tm, wy.shape[1]), lambda i: (i, 0))] * 2,
        out_shape=[out, out],
        compiler_params=_params("parallel"),
        name="lru_proj",
    )(x, gn.reshape(1, d), wy.astype(BF16), wx.astype(BF16))


def _shift_rows(x, d, fill, row):
    return jnp.where(row < d, fill, pltpu.roll(x, d, 0))


def _lru_kernel(ux_ref, y_ref, cw_ref, cb_ref, gaw_ref, gab_ref, gxw_ref, gxb_ref, lam_ref,
                hi_ref, bi_ref, hy_ref, ho_ref, bo_ref, win_ref, hc_ref, *, geom, conv_k):
    c = pl.program_id(0)
    _, first, last = _chunk_pos(c, geom)
    top = SUBLANES
    tail = conv_k - 1

    @pl.when(first)
    def _():
        win_ref[top - tail:top] = bi_ref[0]
        hc_ref[...] = hi_ref[0]

    win_ref[top:] = ux_ref[...]
    u = cb_ref[...] + cw_ref[0:1] * win_ref[top - tail:top - tail + CHUNK]
    for j in range(1, conv_k):
        u = u + cw_ref[j:j + 1] * win_ref[top - tail + j:top - tail + j + CHUNK]

    @pl.when(last)
    def _():
        bo_ref[0] = win_ref[top + CHUNK - tail:top + CHUNK]

    win_ref[:top] = win_ref[CHUNK:CHUNK + top]

    bw = u.shape[1] // LRU_BLOCKS
    zr, zi = [], []
    for blk in range(LRU_BLOCKS):
        ub = u[:, blk * bw:(blk + 1) * bw].astype(BF16)
        zr.append(jnp.dot(ub, gaw_ref[blk], preferred_element_type=F32))
        zi.append(jnp.dot(ub, gxw_ref[blk], preferred_element_type=F32))
    r = _sigmoid(jnp.concatenate(zr, axis=1) + gab_ref[...])
    i = _sigmoid(jnp.concatenate(zi, axis=1) + gxb_ref[...])
    nl = -lam_ref[...]
    softplus = jnp.maximum(nl, 0.0) + jnp.log1p(jnp.exp(-jnp.abs(nl)))
    log_a = -LRU_C * r * softplus
    a = jnp.exp(log_a)
    bterm = jnp.sqrt(-jnp.tanh(log_a) * (a * a + 1.0)) * (i * u)

    row = lax.broadcasted_iota(jnp.int32, a.shape, 0)
    d = 1
    while d < CHUNK:
        a_s = _shift_rows(a, d, 1.0, row)
        b_s = _shift_rows(bterm, d, 0.0, row)
        bterm = bterm + a * b_s
        a = a * a_s
        d *= 2
    h = a * hc_ref[...] + bterm
    hy_ref[...] = h * y_ref[...]
    hc_ref[...] = h[CHUNK - 1:CHUNK]

    @pl.when(last)
    def _():
        ho_ref[0] = h[CHUNK - 1:CHUNK]


def _lru(ux, y, h_in, buf_in, p, j, geom: Geom):
    t, w = ux.shape
    conv_k = p["lru_conv_w"].shape[1]
    row = pl.BlockSpec((CHUNK, w), lambda c: (c, 0))
    vec = _const_spec((1, w))
    gspec = _const_spec(p["lru_ga_w"].shape[1:])
    hspec = pl.BlockSpec((1, 1, w), _seq_index_map(geom, 3))
    bspec = pl.BlockSpec((1, conv_k - 1, w), _seq_index_map(geom, 3))
    h3 = h_in.reshape(geom.n_seq, 1, w)
    return pl.pallas_call(
        functools.partial(_lru_kernel, geom=geom, conv_k=conv_k),
        grid=(geom.n_chunks,),
        in_specs=[row, row, _const_spec((conv_k, w)), vec, gspec, vec, gspec, vec, vec, hspec, bspec],
        out_specs=[row, hspec, bspec],
        out_shape=[jax.ShapeDtypeStruct((t, w), F32), jax.ShapeDtypeStruct(h3.shape, F32),
                   jax.ShapeDtypeStruct(buf_in.shape, F32)],
        scratch_shapes=[pltpu.VMEM((CHUNK + SUBLANES, w), F32), pltpu.VMEM((1, w), F32)],
        compiler_params=_params(),
        name="lru_scan",
    )(ux, y, p["lru_conv_w"][j], p["lru_conv_b"][j].reshape(1, w), p["lru_ga_w"][j].astype(BF16),
      p["lru_ga_b"][j].reshape(1, w), p["lru_gx_w"][j].astype(BF16), p["lru_gx_b"][j].reshape(1, w),
      p["lru_lam"][j].reshape(1, w), h3, buf_in)


def _lru_layer(x, gn, h_in, buf_in, p, j, geom):
    y, ux = _lru_proj(x, gn, p["lru_wy"][j], p["lru_wx"][j])
    hy, h_out, buf_out = _lru(ux, y, h_in, buf_in, p, j, geom)
    return hy, h_out.reshape(h_in.shape), buf_out


def _conf_proj_kernel(x_ref, gn_ref, w1_ref, b1_ref, u_ref):
    d = u_ref.shape[1]
    h = _rms(x_ref[...], gn_ref[...]).astype(BF16)
    hh = jnp.dot(h, w1_ref[...], preferred_element_type=F32) + b1_ref[...]
    u_ref[...] = hh[:, :d] * _sigmoid(hh[:, d:])


def _conf_proj(x, gn, w1, b1, *, tm=512):
    t, d = x.shape
    row = pl.BlockSpec((tm, d), lambda i: (i, 0))
    return pl.pallas_call(
        _conf_proj_kernel,
        grid=(t // tm,),
        in_specs=[row, _const_spec((1, d)), _const_spec(w1.shape), _const_spec((1, w1.shape[1]))],
        out_specs=row,
        out_shape=jax.ShapeDtypeStruct((t, d), F32),
        compiler_params=_params("parallel"),
        name="conf_proj",
    )(x, gn.reshape(1, d), w1.astype(BF16), b1.reshape(1, -1))


def _conf_kernel(u_ref, w_ref, b_ref, lg_ref, lb_ref, bi_ref, c_ref, bo_ref, win_ref, *, geom, conv_k, top):
    c = pl.program_id(0)
    _, first, last = _chunk_pos(c, geom)
    tail = conv_k - 1

    @pl.when(first)
    def _():
        win_ref[top - tail:top] = bi_ref[0]

    win_ref[top:] = u_ref[...]
    acc = b_ref[...] + w_ref[0:1] * win_ref[top - tail:top - tail + CHUNK]
    for j in range(1, conv_k):
        acc = acc + w_ref[j:j + 1] * win_ref[top - tail + j:top - tail + j + CHUNK]

    @pl.when(last)
    def _():
        bo_ref[0] = win_ref[top + CHUNK - tail:top + CHUNK]

    win_ref[:top] = win_ref[CHUNK:CHUNK + top]
    mu = jnp.mean(acc, axis=-1, keepdims=True)
    xc = acc - mu
    var = jnp.mean(xc * xc, axis=-1, keepdims=True)
    c_ref[...] = _silu(xc * lax.rsqrt(var + EPS) * lg_ref[...] + lb_ref[...])


def _conf(u, buf_in, p, j, geom: Geom):
    t, d = u.shape
    conv_k = p["cf_dw_w"].shape[1]
    top = -(-(conv_k - 1) // SUBLANES) * SUBLANES
    assert top <= CHUNK
    row = pl.BlockSpec((CHUNK, d), lambda c: (c, 0))
    vec = _const_spec((1, d))
    bspec = pl.BlockSpec((1, conv_k - 1, d), _seq_index_map(geom, 3))
    return pl.pallas_call(
        functools.partial(_conf_kernel, geom=geom, conv_k=conv_k, top=top),
        grid=(geom.n_chunks,),
        in_specs=[row, _const_spec((conv_k, d)), vec, vec, vec, bspec],
        out_specs=[row, bspec],
        out_shape=[jax.ShapeDtypeStruct((t, d), F32), jax.ShapeDtypeStruct(buf_in.shape, F32)],
        scratch_shapes=[pltpu.VMEM((CHUNK + top, d), F32)],
        compiler_params=_params(),
        name="conf_conv",
    )(u, p["cf_dw_w"][j], p["cf_dw_b"][j].reshape(1, d), p["cf_ln_g"][j].reshape(1, d),
      p["cf_ln_b"][j].reshape(1, d), buf_in)


def _conf_layer(x, gn, buf_in, p, j, geom):
    u = _conf_proj(x, gn, p["cf_w1"][j], p["cf_b1"][j])
    return _conf(u, buf_in, p, j, geom)


_ARG_NAMES = (
    "x_prompt x_sample state_hgrn state_rwkv state_rwkv_shift state_lru state_lru_conv state_conf_conv "
    "norm_mix norm_ffn norm_final hg_wq hg_wf hg_wi hg_wg hg_gn hg_wo hg_lb rw_mu rw_wr rw_wk rw_wv rw_w0 "
    "rw_w1 rw_w2 rw_a0 rw_a1 rw_a2 rw_g1 rw_g2 rw_kk rw_ka rw_rk rw_ln_g rw_ln_b rw_wo lru_wy lru_wx "
    "lru_conv_w lru_conv_b lru_ga_w lru_ga_b lru_gx_w lru_gx_b lru_lam lru_wo cf_w1 cf_b1 cf_dw_w cf_dw_b "
    "cf_ln_g cf_ln_b cf_w2 cf_b2 ffn_w1 ffn_w3 ffn_w2").split()
N_MIXERS = 4


def kernel(x_prompt, x_sample, state_hgrn, state_rwkv, state_rwkv_shift, state_lru, state_lru_conv, state_conf_conv, norm_mix, norm_ffn, norm_final, hg_wq, hg_wf, hg_wi, hg_wg, hg_gn, hg_wo, hg_lb, rw_mu, rw_wr, rw_wk, rw_wv, rw_w0, rw_w1, rw_w2, rw_a0, rw_a1, rw_a2, rw_g1, rw_g2, rw_kk, rw_ka, rw_rk, rw_ln_g, rw_ln_b, rw_wo, lru_wy, lru_wx, lru_conv_w, lru_conv_b, lru_ga_w, lru_ga_b, lru_gx_w, lru_gx_b, lru_lam, lru_wo, cf_w1, cf_b1, cf_dw_w, cf_dw_b, cf_ln_g, cf_ln_b, cf_w2, cf_b2, ffn_w1, ffn_w3, ffn_w2):
    p = dict(zip(_ARG_NAMES, (x_prompt, x_sample, state_hgrn, state_rwkv, state_rwkv_shift, state_lru, state_lru_conv, state_conf_conv, norm_mix, norm_ffn, norm_final, hg_wq, hg_wf, hg_wi, hg_wg, hg_gn, hg_wo, hg_lb, rw_mu, rw_wr, rw_wk, rw_wv, rw_w0, rw_w1, rw_w2, rw_a0, rw_a1, rw_a2, rw_g1, rw_g2, rw_kk, rw_ka, rw_rk, rw_ln_g, rw_ln_b, rw_wo, lru_wy, lru_wx, lru_conv_w, lru_conv_b, lru_ga_w, lru_ga_b, lru_gx_w, lru_gx_b, lru_lam, lru_wo, cf_w1, cf_b1, cf_dw_w, cf_dw_b, cf_ln_g, cf_ln_b, cf_w2, cf_b2, ffn_w1, ffn_w3, ffn_w2)))
    nb_p, seq_p, d = x_prompt.shape
    nb_s, seq_s, _ = x_sample.shape
    assert seq_p % CHUNK == 0 and seq_s % CHUNK == 0
    npc = nb_p * seq_p // CHUNK
    geom = Geom(n_chunks=npc + nb_s * seq_s // CHUNK, npc=npc, cps_p=seq_p // CHUNK,
                cps_s=seq_s // CHUNK, nb_p=nb_p, n_seq=nb_p + nb_s)
    x = jnp.concatenate([x_prompt.reshape(-1, d), x_sample.reshape(-1, d)], axis=0)

    def all_seqs(state):
        return jnp.concatenate([jnp.zeros((nb_p,) + state.shape[1:], state.dtype), state], axis=0)

    depth = norm_mix.shape[0]
    zero_bias = jnp.zeros((d,), F32)
    outs = {k: [] for k in ("hg", "rw", "sh", "lh", "lc", "cf")}
    for i in range(depth):
        m, j = i % N_MIXERS, i // N_MIXERS
        gn = norm_mix[i]
        bo = zero_bias
        if m == 0:
            y, s = _hgrn_layer(x, gn, all_seqs(state_hgrn[j]), p, j, geom)
            outs["hg"].append(s)
            wo = hg_wo[j]
        elif m == 1:
            y, s, sh = _rwkv_layer(x, gn, all_seqs(state_rwkv[j]), all_seqs(state_rwkv_shift[j]), p, j, geom)
            outs["rw"].append(s)
            outs["sh"].append(sh)
            wo = rw_wo[j]
        elif m == 2:
            y, hl, cb = _lru_layer(x, gn, all_seqs(state_lru[j]), all_seqs(state_lru_conv[j]), p, j, geom)
            outs["lh"].append(hl)
            outs["lc"].append(cb)
            wo = lru_wo[j]
        else:
            y, cb = _conf_layer(x, gn, all_seqs(state_conf_conv[j]), p, j, geom)
            outs["cf"].append(cb)
            wo, bo = cf_w2[j], cf_b2[j]
        x = _post_ffn(x, y, wo, bo, norm_ffn[i], ffn_w1[i], ffn_w3[i], ffn_w2[i], norm_final,
                      final=(i == depth - 1))

    n_p = nb_p * seq_p
    stacked = [jnp.stack(outs[k]) for k in ("hg", "rw", "sh", "lh", "lc", "cf")]
    return ((x[:n_p].reshape(nb_p, seq_p, d), x[n_p:].reshape(nb_s, seq_s, d))
            + tuple(s[:, :nb_p] for s in stacked) + tuple(s[:, nb_p:] for s in stacked))
```

```python
import functools
from typing import NamedTuple

import numpy as np
import jax
import jax.numpy as jnp
from jax import lax
from jax.experimental import pallas as pl
from jax.experimental.pallas import tpu as pltpu

F32 = jnp.float32
BF16 = jnp.bfloat16

EPS = 1e-6
CHUNK = 64
SUB = 16
LANES = 128
SUBLANES = 8
HG_DK = 128
RW_N = 64
RW_DECAY_SCALE = 0.6065306597126334
RW_GN_EPS = 64e-5
LRU_BLOCKS = 8
LRU_C = 8.0
VMEM_LIMIT = 56 * 1024 * 1024


class Geom(NamedTuple):
    n_chunks: int
    npc: int
    cps_p: int
    cps_s: int
    nb_p: int
    n_seq: int


def _chunk_pos(c, g: Geom):
    in_p = c < g.npc
    cs = c - g.npc
    seq = jnp.where(in_p, c // g.cps_p, g.nb_p + cs // g.cps_s)
    first = jnp.where(in_p, c % g.cps_p == 0, cs % g.cps_s == 0)
    last = jnp.where(in_p, c % g.cps_p == g.cps_p - 1, cs % g.cps_s == g.cps_s - 1)
    return seq, first, last


def _seq_index_map(g: Geom, ndim):
    def index_map(c):
        seq, _, _ = _chunk_pos(c, g)
        return (seq,) + (0,) * (ndim - 1)
    return index_map


def _dot(a, b):
    return jnp.dot(a.astype(BF16), b.astype(BF16), preferred_element_type=F32)


def _dot_nt(a, b):
    return lax.dot_general(a.astype(BF16), b.astype(BF16), (((1,), (1,)), ((), ())),
                           preferred_element_type=F32)


def _dot_tn(a, b):
    return lax.dot_general(a.astype(BF16), b.astype(BF16), (((0,), (0,)), ((), ())),
                           preferred_element_type=F32)


def _split3(x):
    hi = x.astype(BF16)
    r1 = x - hi.astype(F32)
    mid = r1.astype(BF16)
    lo = (r1 - mid.astype(F32)).astype(BF16)
    return hi, mid, lo


def _dot_sel(sel_bf16, x):
    hi, mid, lo = _split3(x)
    d = lambda t: jnp.dot(sel_bf16, t, preferred_element_type=F32)
    return d(hi) + d(mid) + d(lo)


def _dot_sel2_r(x, sel_bf16):
    hi = x.astype(BF16)
    lo = (x - hi.astype(F32)).astype(BF16)
    return (jnp.dot(hi, sel_bf16, preferred_element_type=F32)
            + jnp.dot(lo, sel_bf16, preferred_element_type=F32))


def _rms(x, g):
    return x * lax.rsqrt(jnp.mean(x * x, axis=-1, keepdims=True) + EPS) * g


def _sigmoid(x):
    return 1.0 / (1.0 + jnp.exp(-x))


def _silu(x):
    return x * _sigmoid(x)


def _tri_incl(n):
    r = lax.broadcasted_iota(jnp.int32, (n, n), 0)
    c = lax.broadcasted_iota(jnp.int32, (n, n), 1)
    return (r >= c).astype(BF16)


def _const_spec(shape):
    return pl.BlockSpec(shape, lambda *_: (0,) * len(shape), pipeline_mode=pl.Buffered(1))


def _params(sem="arbitrary"):
    return pltpu.CompilerParams(dimension_semantics=(sem,), vmem_limit_bytes=VMEM_LIMIT)


def _post_ffn_kernel(x_ref, y_ref, wo_ref, bo_ref, gn_ref, w1_ref, w3_ref, w2_ref, gf_ref, o_ref, *, final):
    x1 = x_ref[...] + _dot(y_ref[...], wo_ref[...]) + bo_ref[...]
    h = _rms(x1, gn_ref[...]).astype(BF16)
    a = jnp.dot(h, w1_ref[...], preferred_element_type=F32)
    b = jnp.dot(h, w3_ref[...], preferred_element_type=F32)
    out = x1 + _dot(_silu(a) * b, w2_ref[...])
    if final:
        out = _rms(out, gf_ref[...])
    o_ref[...] = out


def _post_ffn(x, y, wo, bo, gn, w1, w3, w2, gf, *, final, tm=512):
    t, d = x.shape
    assert t % tm == 0
    row = pl.BlockSpec((tm, d), lambda i: (i, 0))
    return pl.pallas_call(
        functools.partial(_post_ffn_kernel, final=final),
        grid=(t // tm,),
        in_specs=[row, pl.BlockSpec((tm, y.shape[1]), lambda i: (i, 0)),
                  _const_spec(wo.shape), _const_spec((1, d)), _const_spec((1, d)),
                  _const_spec(w1.shape), _const_spec(w3.shape), _const_spec(w2.shape),
                  _const_spec((1, d))],
        out_specs=row,
        out_shape=jax.ShapeDtypeStruct((t, d), F32),
        compiler_params=_params("parallel"),
        name="post_ffn",
    )(x, y, wo.astype(BF16), bo.reshape(1, d), gn.reshape(1, d), w1.astype(BF16), w3.astype(BF16),
      w2.astype(BF16), gf.reshape(1, d))


def _hgrn_proj_kernel(x_ref, gn_ref, lbp_ref, wq_ref, wf_ref, wi_ref, wg_ref,
                      q_ref, lk_ref, lf_ref, v_ref, gate_ref, *, layer):
    h = _rms(x_ref[...], gn_ref[...]).astype(BF16)
    lbp = lbp_ref[...]
    e = jnp.exp(lbp - jnp.max(lbp, axis=0, keepdims=True))
    lb = jnp.sum(e[:layer + 1], axis=0, keepdims=True) / jnp.sum(e, axis=0, keepdims=True)
    q_ref[...] = _silu(jnp.dot(h, wq_ref[...], preferred_element_type=F32)) * (HG_DK ** -0.5)
    fl = jnp.dot(h, wf_ref[...], preferred_element_type=F32)
    t = jnp.exp(-jnp.abs(fl))
    sig = jnp.where(fl >= 0.0, 1.0, t) / (1.0 + t)
    lk_ref[...] = jnp.log(1.0 - lb) - (jnp.maximum(fl, 0.0) + jnp.log1p(t))
    lf_ref[...] = jnp.log(lb + (1.0 - lb) * sig)
    v_ref[...] = jnp.dot(h, wi_ref[...], preferred_element_type=F32)
    gate_ref[...] = _silu(jnp.dot(h, wg_ref[...], preferred_element_type=F32))


def _hgrn_proj(x, gn, lbp, wq, wf, wi, wg, *, layer, tm=512):
    t, d = x.shape
    row = pl.BlockSpec((tm, d), lambda i: (i, 0))
    wspec = _const_spec((d, d))
    out = jax.ShapeDtypeStruct((t, d), F32)
    return pl.pallas_call(
        functools.partial(_hgrn_proj_kernel, layer=layer),
        grid=(t // tm,),
        in_specs=[row, _const_spec((1, d)), _const_spec(lbp.shape), wspec, wspec, wspec, wspec],
        out_specs=[row] * 5,
        out_shape=[out] * 5,
        compiler_params=_params("parallel"),
        name="hgrn_proj",
    )(x, gn.reshape(1, d), lbp, wq.astype(BF16), wf.astype(BF16), wi.astype(BF16), wg.astype(BF16))


def _gla_kernel(q_ref, lk_ref, lf_ref, v_ref, gate_ref, gn_ref, si_ref, og_ref, so_ref,
                st_ref, zp_ref, *, geom, n_heads):
    c = pl.program_id(0)
    _, first, last = _chunk_pos(c, geom)

    @pl.when(first)
    def _():
        for h in range(n_heads):
            st_ref[h] = si_ref[0, h].T

    cum = _dot_sel(_tri_incl(CHUNK), lf_ref[...])
    z_all = lk_ref[...] - cum
    zp_ref[:SUB] = jnp.zeros((SUB, zp_ref.shape[1]), F32)
    zp_ref[SUB:] = z_all
    ones = jnp.ones((HG_DK, LANES), BF16)
    t_idx = lax.broadcasted_iota(jnp.int32, (CHUNK, LANES), 0)
    s_idx = lax.broadcasted_iota(jnp.int32, (CHUNK, LANES), 1)
    pair_delta = jnp.where((s_idx <= t_idx) & (s_idx // SUB == t_idx // SUB), t_idx - s_idx, -1)
    n_sub = CHUNK // SUB
    batch = 4

    heads = range(n_heads)
    sls = [slice(h * HG_DK, (h + 1) * HG_DK) for h in heads]
    q_all, v_all = q_ref[...], v_ref[...]
    q_in = q_all * jnp.exp(cum)
    k_out = jnp.exp(z_all + cum[CHUNK - 1:CHUNK])
    o = [_dot_nt(q_in[:, s], st_ref[h]) for h, s in zip(heads, sls)]
    upd = [_dot_tn(v_all[:, s], k_out[:, s]) for s in sls]
    blocks = [[jnp.zeros((SUB, HG_DK), F32)] for _ in heads]
    for i in range(1, n_sub):
        lo, hi = i * SUB, (i + 1) * SUB
        ref = cum[lo - 1:lo]
        qi = q_all[lo:hi] * jnp.exp(cum[lo:hi] - ref)
        kj = jnp.exp(z_all[:lo] + ref)
        sc = [_dot_nt(qi[:, s], kj[:, s]) for s in sls]
        for h in heads:
            blocks[h].append(_dot(sc[h], v_all[:lo, sls[h]]))
    o = [o[h] + jnp.concatenate(blocks[h], axis=0) for h in heads]
    scores = [jnp.zeros((CHUNK, LANES), F32) for _ in heads]
    for b0 in range(0, SUBLANES, batch):
        deltas = [d for b in range(b0, b0 + batch) for d in (b, b + SUBLANES)]
        sums = []
        for h, s in zip(heads, sls):
            parts = []
            for b in range(b0, b0 + batch):
                zb = zp_ref[SUBLANES - b:SUBLANES - b + CHUNK + SUBLANES, s]
                for zs in (zb[SUBLANES:], zb[:CHUNK]):
                    parts.append((q_all[:, s] * jnp.exp(cum[:, s] + zs)).astype(BF16))
            sums.append(jnp.dot(jnp.concatenate(parts, axis=0), ones, preferred_element_type=F32))
        for h in heads:
            for i, delta in enumerate(deltas):
                scores[h] = jnp.where(pair_delta == delta, sums[h][i * CHUNK:(i + 1) * CHUNK], scores[h])
    o = [o[h] + _dot(scores[h][:, :CHUNK], v_all[:, sls[h]]) for h in heads]
    decay = jnp.exp(cum[CHUNK - 1:CHUNK])
    for h, s in zip(heads, sls):
        st_ref[h] = st_ref[h] * decay[:, s] + upd[h]
        on = o[h] * lax.rsqrt(jnp.mean(o[h] * o[h], axis=-1, keepdims=True) + EPS) * gn_ref[...]
        og_ref[:, s] = on * gate_ref[:, s]

    @pl.when(last)
    def _():
        for h in range(n_heads):
            so_ref[0, h] = st_ref[h].T


def _gla(q, lk, lf, v, gate, gn, s_in, geom: Geom):
    t, d = q.shape
    n_heads = d // HG_DK
    row = pl.BlockSpec((CHUNK, d), lambda c: (c, 0))
    sspec = pl.BlockSpec((1,) + s_in.shape[1:], _seq_index_map(geom, s_in.ndim))
    return pl.pallas_call(
        functools.partial(_gla_kernel, geom=geom, n_heads=n_heads),
        grid=(geom.n_chunks,),
        in_specs=[row] * 5 + [_const_spec((1, HG_DK)), sspec],
        out_specs=[row, sspec],
        out_shape=[jax.ShapeDtypeStruct((t, d), F32), jax.ShapeDtypeStruct(s_in.shape, F32)],
        scratch_shapes=[pltpu.VMEM((n_heads, HG_DK, HG_DK), F32), pltpu.VMEM((CHUNK + SUB, d), F32)],
        compiler_params=_params(),
        name="hgrn_gla",
    )(q, lk, lf, v, gate, gn.reshape(1, HG_DK), s_in)


def _hgrn_layer(x, gn, s_in, p, j, geom):
    q, lk, lf, v, gate = _hgrn_proj(x, gn, p["hg_lb"], p["hg_wq"][j], p["hg_wf"][j],
                                    p["hg_wi"][j], p["hg_wg"][j], layer=j)
    return _gla(q, lk, lf, v, gate, p["hg_gn"][j], s_in, geom)


def _rwkv_proj_kernel(x_ref, xp_ref, st_ref, gn_ref, mu_ref, wr_ref, wk_ref, wv_ref, w0_ref, w1_ref,
                      w2_ref, a0_ref, a1_ref, a2_ref, g1_ref, g2_ref, kk_ref, ka_ref,
                      r_ref, lw_ref, kh_ref, v_ref, kkr_ref, a_ref, g_ref, hl_ref,
                      win_ref, *, geom, tm):
    i = pl.program_id(0)
    cpt = tm // CHUNK
    gn = gn_ref[...]
    h = _rms(x_ref[...], gn)
    win_ref[SUBLANES:] = h
    win_ref[:SUBLANES] = _rms(xp_ref[...], gn)
    for j in range(cpt):
        last_row = SUBLANES + (j + 1) * CHUNK - 1
        hl_ref[j] = win_ref[last_row:last_row + 1]
    for j in range(cpt):
        _, first, _ = _chunk_pos(i * cpt + j, geom)

        @pl.when(first)
        def _():
            row = SUBLANES - 1 + j * CHUNK
            win_ref[row:row + 1] = st_ref[j]

    xx = win_ref[SUBLANES - 1:SUBLANES - 1 + tm] - h
    mix = lambda n: (h + xx * mu_ref[n:n + 1]).astype(BF16)
    dot = lambda a, w_ref: jnp.dot(a, w_ref[...], preferred_element_type=F32)
    r_ref[...] = dot(mix(0), wr_ref)
    k = dot(mix(2), wk_ref)
    v_ref[...] = dot(mix(3), wv_ref)
    tw = jnp.tanh(dot(mix(1), w1_ref)).astype(BF16)
    lw_ref[...] = -RW_DECAY_SCALE * _sigmoid(w0_ref[...] + dot(tw, w2_ref))
    a = _sigmoid(a0_ref[...] + dot(dot(mix(4), a1_ref).astype(BF16), a2_ref))
    g_ref[...] = dot(_sigmoid(dot(mix(5), g1_ref)).astype(BF16), g2_ref)
    a_ref[...] = a
    kkr_ref[...] = k * kk_ref[...]
    kh_ref[...] = k * (1.0 + (a - 1.0) * ka_ref[...])


def _rwkv_proj(x, gn, start_rows, p, j, geom, *, tm=256):
    t, d = x.shape
    cpt = tm // CHUNK
    row = pl.BlockSpec((tm, d), lambda i: (i, 0))
    prev = pl.BlockSpec((SUBLANES, d), lambda i: (jnp.maximum(i * (tm // SUBLANES) - 1, 0), 0))
    vec = _const_spec((1, d))
    bf = lambda w: w.astype(BF16)
    ws = [bf(p["rw_wr"][j]), bf(p["rw_wk"][j]), bf(p["rw_wv"][j]), p["rw_w0"][j].reshape(1, d),
          bf(p["rw_w1"][j]), bf(p["rw_w2"][j]), p["rw_a0"][j].reshape(1, d), bf(p["rw_a1"][j]),
          bf(p["rw_a2"][j]), bf(p["rw_g1"][j]), bf(p["rw_g2"][j]), p["rw_kk"][j].reshape(1, d),
          p["rw_ka"][j].reshape(1, d)]
    out = jax.ShapeDtypeStruct((t, d), F32)
    return pl.pallas_call(
        functools.partial(_rwkv_proj_kernel, geom=geom, tm=tm),
        grid=(t // tm,),
        in_specs=[row, prev, pl.BlockSpec((cpt, 1, d), lambda i: (i, 0, 0)), vec,
                  _const_spec((6, d))] + [_const_spec(w.shape) for w in ws],
        out_specs=[row] * 7 + [pl.BlockSpec((cpt, 1, d), lambda i: (i, 0, 0))],
        out_shape=[out] * 7 + [jax.ShapeDtypeStruct((geom.n_chunks, 1, d), F32)],
        scratch_shapes=[pltpu.VMEM((tm + SUBLANES, d), F32)],
        compiler_params=_params(),
        name="rwkv_proj",
    )(x, x, start_rows.reshape(geom.n_chunks, 1, d), gn.reshape(1, d), p["rw_mu"][j], *ws)


def _block_diag(x, bd_mask):
    return jnp.where(bd_mask, jnp.concatenate([x, x], axis=0), 0.0)


def _rwkv_kernel(r_ref, lw_ref, kh_ref, v_ref, kkr_ref, a_ref, g_ref, rk_ref, lg_ref, lb_ref, si_ref,
                 y_ref, so_ref, st_ref, *, geom, n_pairs):
    c = pl.program_id(0)
    _, first, last = _chunk_pos(c, geom)
    n = RW_N
    p2 = 2 * n

    @pl.when(first)
    def _():
        z = jnp.zeros((n, n), F32)
        for p in range(n_pairs):
            sa, sb = si_ref[0, 2 * p], si_ref[0, 2 * p + 1]
            st_ref[p] = jnp.concatenate([jnp.concatenate([sa, z], axis=1),
                                         jnp.concatenate([z, sb], axis=1)], axis=0)

    tri = _tri_incl(CHUNK)
    lw = lw_ref[...]
    cw = _dot_sel(tri, lw)
    w_inc = jnp.exp(cw)
    w_exc = jnp.exp(cw - lw)
    w_inv = jnp.exp(-cw)
    w_end = jnp.exp(cw[CHUNK - 1:CHUNK] - cw)

    ri = lax.broadcasted_iota(jnp.int32, (p2, p2), 0)
    ci = lax.broadcasted_iota(jnp.int32, (p2, p2), 1)
    bd_mask = (ri // n) == (ci // n)
    ones_bd = bd_mask.astype(BF16)
    t_idx = lax.broadcasted_iota(jnp.int32, (CHUNK, p2), 0)
    s_idx = lax.broadcasted_iota(jnp.int32, (CHUNK, p2), 1) % n
    strict = s_idx < t_idx
    incl = s_idx <= t_idx
    eye = (s_idx == t_idx).astype(F32)
    bd = lambda x: _block_diag(x, bd_mask)

    pairs = range(n_pairs)
    sls = [slice(p * p2, (p + 1) * p2) for p in pairs]
    def segsum(xs):
        tot = _dot_sel2_r(jnp.concatenate(xs, axis=0), ones_bd)
        return [tot[i * CHUNK:(i + 1) * CHUNK] for i in range(len(xs))]

    r, kh, v, kkr = r_ref[...], kh_ref[...], v_ref[...], kkr_ref[...]
    rkk = r * kh * rk_ref[...]
    sums = segsum([(kkr * kkr)[:, s] for s in sls] + [rkk[:, s] for s in sls])
    ss = jnp.concatenate(sums[:n_pairs], axis=1)
    bonus = [sums[n_pairs + p] * v[:, sls[p]] for p in pairs]
    kk = kkr / jnp.maximum(jnp.sqrt(ss), 1e-12)
    b = kk * a_ref[...]
    kk_t, r_t = kk * w_exc, r * w_inc
    b_t, k_t = b * w_inv, kh * w_inv
    k_e, b_e = kh * w_end, b * w_end
    st =[st_ref[p] for p in pairs]
    bd_v = [bd(v[:, s]).astype(BF16) for s in sls]
    lhs = [jnp.concatenate([kk_t[:, s], r_t[:, s]], axis=0) for s in sls]
    g_b = [_dot_nt(lhs[p], bd(b_t[:, sls[p]])) for p in pairs]
    g_k = [_dot_nt(lhs[p], bd(k_t[:, sls[p]])) for p in pairs]
    from_state = [_dot_nt(lhs[p], st[p]) for p in pairs]
    rhs = [from_state[p][:CHUNK] + _dot(jnp.where(strict, g_k[p][:CHUNK], 0.0), bd_v[p]) for p in pairs]
    o_kv = [from_state[p][CHUNK:] + _dot(jnp.where(incl, g_k[p][CHUNK:], 0.0), bd_v[p]) for p in pairs]
    npow = [jnp.where(strict, -g_b[p][:CHUNK], 0.0) for p in pairs]
    tinv = [eye + npow[p] for p in pairs]
    npow = [_dot(npow[p], bd(npow[p])) for p in pairs]
    for _ in range(4):
        both = [_dot(jnp.concatenate([npow[p], tinv[p]], axis=0), bd(npow[p])) for p in pairs]
        npow = [both[p][:CHUNK] for p in pairs]
        tinv = [tinv[p] + both[p][CHUNK:] for p in pairs]
    tinv = [tinv[p] + _dot(tinv[p], bd(npow[p])) for p in pairs]
    u = [_dot(tinv[p], bd(rhs[p])) for p in pairs]
    o = [o_kv[p] - _dot(jnp.where(incl, g_b[p][CHUNK:], 0.0), bd(u[p])) for p in pairs]
    for p in pairs:
        s = sls[p]
        upd = _dot_tn(jnp.concatenate([v[:, s], -u[p]], axis=0),
                      jnp.concatenate([k_e[:, s], b_e[:, s]], axis=0))
        st_ref[p] = st[p] * w_inc[CHUNK - 1:CHUNK, s] + jnp.where(bd_mask, upd, 0.0)

    mean = segsum(o)
    oc = [o[p] - mean[p] * (1.0 / n) for p in pairs]
    var = segsum([oc[p] * oc[p] for p in pairs])
    for p in pairs:
        s = sls[p]
        on = oc[p] * lax.rsqrt(var[p] * (1.0 / n) + RW_GN_EPS) * lg_ref[:, s] + lb_ref[:, s]
        y_ref[:, s] = (on + bonus[p]) * g_ref[:, s]

    @pl.when(last)
    def _():
        for p in pairs:
            so_ref[0, 2 * p] = st_ref[p, :n, :n]
            so_ref[0, 2 * p + 1] = st_ref[p, n:, n:]


def _rwkv(r, lw, kh, v, kkr, a, g, rk, lg, lb, s_in, geom: Geom):
    t, d = r.shape
    n_pairs = d // (2 * RW_N)
    row = pl.BlockSpec((CHUNK, d), lambda c: (c, 0))
    vec = _const_spec((1, d))
    sspec = pl.BlockSpec((1,) + s_in.shape[1:], _seq_index_map(geom, s_in.ndim))
    return pl.pallas_call(
        functools.partial(_rwkv_kernel, geom=geom, n_pairs=n_pairs),
        grid=(geom.n_chunks,),
        in_specs=[row] * 7 + [vec] * 3 + [sspec],
        out_specs=[row, sspec],
        out_shape=[jax.ShapeDtypeStruct((t, d), F32), jax.ShapeDtypeStruct(s_in.shape, F32)],
        scratch_shapes=[pltpu.VMEM((n_pairs, 2 * RW_N, 2 * RW_N), F32)],
        compiler_params=_params(),
        name="rwkv_scan",
    )(r, lw, kh, v, kkr, a, g, rk.reshape(1, d), lg.reshape(1, d), lb.reshape(1, d), s_in)


def _rwkv_layer(x, gn, s_in, shift_in, p, j, geom):
    d = x.shape[1]
    seq_first = _first_chunks(geom)
    start_rows = jnp.zeros((geom.n_chunks, d), F32).at[seq_first].set(shift_in)
    r, lw, kh, v, kkr, a, g, hl = _rwkv_proj(x, gn, start_rows, p, j, geom)
    y, s_out = _rwkv(r, lw, kh, v, kkr, a, g, p["rw_rk"][j], p["rw_ln_g"][j], p["rw_ln_b"][j], s_in, geom)
    return y, s_out, hl.reshape(geom.n_chunks, d)[_last_chunks(geom)]


def _first_chunks(g: Geom):
    return np.concatenate([np.arange(g.nb_p) * g.cps_p,
                           g.npc + np.arange(g.n_seq - g.nb_p) * g.cps_s])


def _last_chunks(g: Geom):
    return _first_chunks(g) + np.concatenate([np.full(g.nb_p, g.cps_p - 1),
                                              np.full(g.n_seq - g.nb_p, g.cps_s - 1)])


def _gelu_tanh(x):
    return 0.5 * x * (1.0 + jnp.tanh(0.7978845608028654 * (x + 0.044715 * x * x * x)))


def _lru_proj_kernel(x_ref, gn_ref, wy_ref, wx_ref, y_ref, ux_ref):
    h = _rms(x_ref[...], gn_ref[...]).astype(BF16)
    y_ref[...] = _gelu_tanh(jnp.dot(h, wy_ref[...], preferred_element_type=F32))
    ux_ref[...] = jnp.dot(h, wx_ref[...], preferred_element_type=F32)


def _lru_proj(x, gn, wy, wx, *, tm=512):
    t, d = x.shape
    row = pl.BlockSpec((tm, d), lambda i: (i, 0))
    out = jax.ShapeDtypeStruct((t, wy.shape[1]), F32)
    return pl.pallas_call(
        _lru_proj_kernel,
        grid=(t // tm,),
        in_specs=[row, _const_spec((1, d)), _const_spec(wy.shape), _const_spec(wx.shape)],
        out_specs=[pl.BlockSpec((tm, wy.shape[1]), lambda i: (i, 0))] * 2,
        out_shape=[out, out],
        compiler_params=_params("parallel"),
        name="lru_proj",
    )(x, gn.reshape(1, d), wy.astype(BF16), wx.astype(BF16))


def _shift_rows(x, d, fill, row):
    return jnp.where(row < d, fill, pltpu.roll(x, d, 0))


def _lru_kernel(ux_ref, y_ref, cw_ref, cb_ref, gaw_ref, gab_ref, gxw_ref, gxb_ref, lam_ref,
                hi_ref, bi_ref, hy_ref, ho_ref, bo_ref, win_ref, hc_ref, *, geom, conv_k):
    c = pl.program_id(0)
    _, first, last = _chunk_pos(c, geom)
    top = SUBLANES
    tail = conv_k - 1

    @pl.when(first)
    def _():
        win_ref[top - tail:top] = bi_ref[0]
        hc_ref[...] = hi_ref[0]

    win_ref[top:] = ux_ref[...]
    _window_conv(win_ref, cw_ref, cb_ref, hy_ref, top - tail, conv_k)
    u = hy_ref[...]

    @pl.when(last)
    def _():
        bo_ref[0] = win_ref[top + CHUNK - tail:top + CHUNK]

    win_ref[:top] = win_ref[CHUNK:CHUNK + top]

    bw = u.shape[1] // LRU_BLOCKS
    zr, zi = [], []
    for blk in range(LRU_BLOCKS):
        ub = u[:, blk * bw:(blk + 1) * bw].astype(BF16)
        zr.append(jnp.dot(ub, gaw_ref[blk], preferred_element_type=F32))
        zi.append(jnp.dot(ub, gxw_ref[blk], preferred_element_type=F32))
    r = _sigmoid(jnp.concatenate(zr, axis=1) + gab_ref[...])
    i = _sigmoid(jnp.concatenate(zi, axis=1) + gxb_ref[...])
    nl = -lam_ref[...]
    softplus = jnp.maximum(nl, 0.0) + jnp.log1p(jnp.exp(-jnp.abs(nl)))
    log_a = -LRU_C * r * softplus
    a = jnp.exp(log_a)
    bterm = jnp.sqrt(-jnp.tanh(log_a) * (a * a + 1.0)) * (i * u)

    row = lax.broadcasted_iota(jnp.int32, a.shape, 0)
    d = 1
    while d < CHUNK:
        a_s = _shift_rows(a, d, 1.0, row)
        b_s = _shift_rows(bterm, d, 0.0, row)
        bterm = bterm + a * b_s
        a = a * a_s
        d *= 2
    h = a * hc_ref[...] + bterm
    hy_ref[...] = h * y_ref[...]
    hc_ref[...] = h[CHUNK - 1:CHUNK]

    @pl.when(last)
    def _():
        ho_ref[0] = h[CHUNK - 1:CHUNK]


def _lru(ux, y, h_in, buf_in, p, j, geom: Geom):
    t, w = ux.shape
    conv_k = p["lru_conv_w"].shape[1]
    row = pl.BlockSpec((CHUNK, w), lambda c: (c, 0))
    vec = _const_spec((1, w))
    gspec = _const_spec(p["lru_ga_w"].shape[1:])
    hspec = pl.BlockSpec((1, 1, w), _seq_index_map(geom, 3))
    bspec = pl.BlockSpec((1, conv_k - 1, w), _seq_index_map(geom, 3))
    h3 = h_in.reshape(geom.n_seq, 1, w)
    return pl.pallas_call(
        functools.partial(_lru_kernel, geom=geom, conv_k=conv_k),
        grid=(geom.n_chunks,),
        in_specs=[row, row, _const_spec((conv_k, w)), vec, gspec, vec, gspec, vec, vec, hspec, bspec],
        out_specs=[row, hspec, bspec],
        out_shape=[jax.ShapeDtypeStruct((t, w), F32), jax.ShapeDtypeStruct(h3.shape, F32),
                   jax.ShapeDtypeStruct(buf_in.shape, F32)],
        scratch_shapes=[pltpu.VMEM((CHUNK + SUBLANES, w), F32), pltpu.VMEM((1, w), F32)],
        compiler_params=_params(),
        name="lru_scan",
    )(ux, y, p["lru_conv_w"][j], p["lru_conv_b"][j].reshape(1, w), p["lru_ga_w"][j].astype(BF16),
      p["lru_ga_b"][j].reshape(1, w), p["lru_gx_w"][j].astype(BF16), p["lru_gx_b"][j].reshape(1, w),
      p["lru_lam"][j].reshape(1, w), h3, buf_in)


def _lru_layer(x, gn, h_in, buf_in, p, j, geom):
    y, ux = _lru_proj(x, gn, p["lru_wy"][j], p["lru_wx"][j])
    hy, h_out, buf_out = _lru(ux, y, h_in, buf_in, p, j, geom)
    return hy, h_out.reshape(h_in.shape), buf_out


def _conf_proj_kernel(x_ref, gn_ref, w1_ref, b1_ref, u_ref):
    d = u_ref.shape[1]
    h = _rms(x_ref[...], gn_ref[...]).astype(BF16)
    hh = jnp.dot(h, w1_ref[...], preferred_element_type=F32) + b1_ref[...]
    u_ref[...] = hh[:, :d] * _sigmoid(hh[:, d:])


def _conf_proj(x, gn, w1, b1, *, tm=512):
    t, d = x.shape
    row = pl.BlockSpec((tm, d), lambda i: (i, 0))
    return pl.pallas_call(
        _conf_proj_kernel,
        grid=(t // tm,),
        in_specs=[row, _const_spec((1, d)), _const_spec(w1.shape), _const_spec((1, w1.shape[1]))],
        out_specs=row,
        out_shape=jax.ShapeDtypeStruct((t, d), F32),
        compiler_params=_params("parallel"),
        name="conf_proj",
    )(x, gn.reshape(1, d), w1.astype(BF16), b1.reshape(1, -1))


def _window_conv(win_ref, w_ref, b_ref, out_ref, base, conv_k):
    for lt in range(out_ref.shape[1] // LANES):
        sl = slice(lt * LANES, (lt + 1) * LANES)
        acc = jnp.broadcast_to(b_ref[:, sl], (CHUNK, LANES))
        for b in range(SUBLANES):
            offs = [o for o in range(base, base + conv_k) if o % SUBLANES == b]
            if not offs:
                continue
            rows = CHUNK + SUBLANES if b else CHUNK
            part = sum(w_ref[o - base:o - base + 1, sl] * win_ref[o - b:o - b + rows, sl] for o in offs)
            acc = acc + (pltpu.roll(part, rows - b, 0)[:CHUNK] if b else part)
        out_ref[:, sl] = acc


def _conf_kernel(u_ref, w_ref, b_ref, lg_ref, lb_ref, bi_ref, c_ref, bo_ref, win_ref, *, geom, conv_k, top):
    c = pl.program_id(0)
    _, first, last = _chunk_pos(c, geom)
    tail = conv_k - 1

    @pl.when(first)
    def _():
        win_ref[top - tail:top] = bi_ref[0]

    win_ref[top:] = u_ref[...]
    _window_conv(win_ref, w_ref, b_ref, c_ref, top - tail, conv_k)
    acc = c_ref[...]

    @pl.when(last)
    def _():
        bo_ref[0] = win_ref[top + CHUNK - tail:top + CHUNK]

    win_ref[:top] = win_ref[CHUNK:CHUNK + top]
    mu = jnp.mean(acc, axis=-1, keepdims=True)
    xc = acc - mu
    var = jnp.mean(xc * xc, axis=-1, keepdims=True)
    c_ref[...] = _silu(xc * lax.rsqrt(var + EPS) * lg_ref[...] + lb_ref[...])


def _conf(u, buf_in, p, j, geom: Geom):
    t, d = u.shape
    conv_k = p["cf_dw_w"].shape[1]
    top = -(-(conv_k - 1) // SUBLANES) * SUBLANES
    assert top <= CHUNK
    row = pl.BlockSpec((CHUNK, d), lambda c: (c, 0))
    vec = _const_spec((1, d))
    bspec = pl.BlockSpec((1, conv_k - 1, d), _seq_index_map(geom, 3))
    return pl.pallas_call(
        functools.partial(_conf_kernel, geom=geom, conv_k=conv_k, top=top),
        grid=(geom.n_chunks,),
        in_specs=[row, _const_spec((conv_k, d)), vec, vec, vec, bspec],
        out_specs=[row, bspec],
        out_shape=[jax.ShapeDtypeStruct((t, d), F32), jax.ShapeDtypeStruct(buf_in.shape, F32)],
        scratch_shapes=[pltpu.VMEM((CHUNK + top, d), F32)],
        compiler_params=_params(),
        name="conf_conv",
    )(u, p["cf_dw_w"][j], p["cf_dw_b"][j].reshape(1, d), p["cf_ln_g"][j].reshape(1, d),
      p["cf_ln_b"][j].reshape(1, d), buf_in)


def _conf_layer(x, gn, buf_in, p, j, geom):
    u = _conf_proj(x, gn, p["cf_w1"][j], p["cf_b1"][j])
    return _conf(u, buf_in, p, j, geom)


_ARG_NAMES = (
    "x_prompt x_sample state_hgrn state_rwkv state_rwkv_shift state_lru state_lru_conv state_conf_conv "
    "norm_mix norm_ffn norm_final hg_wq hg_wf hg_wi hg_wg hg_gn hg_wo hg_lb rw_mu rw_wr rw_wk rw_wv rw_w0 "
    "rw_w1 rw_w2 rw_a0 rw_a1 rw_a2 rw_g1 rw_g2 rw_kk rw_ka rw_rk rw_ln_g rw_ln_b rw_wo lru_wy lru_wx "
    "lru_conv_w lru_conv_b lru_ga_w lru_ga_b lru_gx_w lru_gx_b lru_lam lru_wo cf_w1 cf_b1 cf_dw_w cf_dw_b "
    "cf_ln_g cf_ln_b cf_w2 cf_b2 ffn_w1 ffn_w3 ffn_w2").split()
N_MIXERS = 4


def kernel(x_prompt, x_sample, state_hgrn, state_rwkv, state_rwkv_shift, state_lru, state_lru_conv, state_conf_conv, norm_mix, norm_ffn, norm_final, hg_wq, hg_wf, hg_wi, hg_wg, hg_gn, hg_wo, hg_lb, rw_mu, rw_wr, rw_wk, rw_wv, rw_w0, rw_w1, rw_w2, rw_a0, rw_a1, rw_a2, rw_g1, rw_g2, rw_kk, rw_ka, rw_rk, rw_ln_g, rw_ln_b, rw_wo, lru_wy, lru_wx, lru_conv_w, lru_conv_b, lru_ga_w, lru_ga_b, lru_gx_w, lru_gx_b, lru_lam, lru_wo, cf_w1, cf_b1, cf_dw_w, cf_dw_b, cf_ln_g, cf_ln_b, cf_w2, cf_b2, ffn_w1, ffn_w3, ffn_w2):
    p = dict(zip(_ARG_NAMES, (x_prompt, x_sample, state_hgrn, state_rwkv, state_rwkv_shift, state_lru, state_lru_conv, state_conf_conv, norm_mix, norm_ffn, norm_final, hg_wq, hg_wf, hg_wi, hg_wg, hg_gn, hg_wo, hg_lb, rw_mu, rw_wr, rw_wk, rw_wv, rw_w0, rw_w1, rw_w2, rw_a0, rw_a1, rw_a2, rw_g1, rw_g2, rw_kk, rw_ka, rw_rk, rw_ln_g, rw_ln_b, rw_wo, lru_wy, lru_wx, lru_conv_w, lru_conv_b, lru_ga_w, lru_ga_b, lru_gx_w, lru_gx_b, lru_lam, lru_wo, cf_w1, cf_b1, cf_dw_w, cf_dw_b, cf_ln_g, cf_ln_b, cf_w2, cf_b2, ffn_w1, ffn_w3, ffn_w2)))
    nb_p, seq_p, d = x_prompt.shape
    nb_s, seq_s, _ = x_sample.shape
    assert seq_p % CHUNK == 0 and seq_s % CHUNK == 0
    npc = nb_p * seq_p // CHUNK
    geom = Geom(n_chunks=npc + nb_s * seq_s // CHUNK, npc=npc, cps_p=seq_p // CHUNK,
                cps_s=seq_s // CHUNK, nb_p=nb_p, n_seq=nb_p + nb_s)
    x = jnp.concatenate([x_prompt.reshape(-1, d), x_sample.reshape(-1, d)], axis=0)

    def all_seqs(state):
        return jnp.concatenate([jnp.zeros((nb_p,) + state.shape[1:], state.dtype), state], axis=0)

    depth = norm_mix.shape[0]
    zero_bias = jnp.zeros((d,), F32)
    outs = {k: [] for k in ("hg", "rw", "sh", "lh", "lc", "cf")}
    for i in range(depth):
        m, j = i % N_MIXERS, i // N_MIXERS
        gn = norm_mix[i]
        bo = zero_bias
        if m == 0:
            y, s = _hgrn_layer(x, gn, all_seqs(state_hgrn[j]), p, j, geom)
            outs["hg"].append(s)
            wo = hg_wo[j]
        elif m == 1:
            y, s, sh = _rwkv_layer(x, gn, all_seqs(state_rwkv[j]), all_seqs(state_rwkv_shift[j]), p, j, geom)
            outs["rw"].append(s)
            outs["sh"].append(sh)
            wo = rw_wo[j]
        elif m == 2:
            y, hl, cb = _lru_layer(x, gn, all_seqs(state_lru[j]), all_seqs(state_lru_conv[j]), p, j, geom)
            outs["lh"].append(hl)
            outs["lc"].append(cb)
            wo = lru_wo[j]
        else:
            y, cb = _conf_layer(x, gn, all_seqs(state_conf_conv[j]), p, j, geom)
            outs["cf"].append(cb)
            wo, bo = cf_w2[j], cf_b2[j]
        x = _post_ffn(x, y, wo, bo, norm_ffn[i], ffn_w1[i], ffn_w3[i], ffn_w2[i], norm_final,
                      final=(i == depth - 1))

    n_p = nb_p * seq_p
    stacked = [jnp.stack(outs[k]) for k in ("hg", "rw", "sh", "lh", "lc", "cf")]
    return ((x[:n_p].reshape(nb_p, seq_p, d), x[n_p:].reshape(nb_s, seq_s, d))
            + tuple(s[:, :nb_p] for s in stacked) + tuple(s[:, nb_p:] for s in stacked))
```

```python
import functools
from typing import NamedTuple

import numpy as np
import jax
import jax.numpy as jnp
from jax import lax
from jax.experimental import pallas as pl
from jax.experimental.pallas import tpu as pltpu

F32 = jnp.float32
BF16 = jnp.bfloat16

EPS = 1e-6
CHUNK = 64
SUB = 16
LANES = 128
SUBLANES = 8
HG_DK = 128
RW_N = 64
RW_DECAY_SCALE = 0.6065306597126334
RW_GN_EPS = 64e-5
LRU_BLOCKS = 8
LRU_C = 8.0
VMEM_LIMIT = 56 * 1024 * 1024


class Geom(NamedTuple):
    n_chunks: int
    npc: int
    cps_p: int
    cps_s: int
    nb_p: int
    n_seq: int


def _chunk_pos(c, g: Geom):
    in_p = c < g.npc
    cs = c - g.npc
    seq = jnp.where(in_p, c // g.cps_p, g.nb_p + cs // g.cps_s)
    first = jnp.where(in_p, c % g.cps_p == 0, cs % g.cps_s == 0)
    last = jnp.where(in_p, c % g.cps_p == g.cps_p - 1, cs % g.cps_s == g.cps_s - 1)
    return seq, first, last


def _seq_index_map(g: Geom, ndim):
    def index_map(c):
        seq, _, _ = _chunk_pos(c, g)
        return (seq,) + (0,) * (ndim - 1)
    return index_map


def _dot(a, b):
    return jnp.dot(a.astype(BF16), b.astype(BF16), preferred_element_type=F32)


def _dot_nt(a, b):
    return lax.dot_general(a.astype(BF16), b.astype(BF16), (((1,), (1,)), ((), ())),
                           preferred_element_type=F32)


def _dot_tn(a, b):
    return lax.dot_general(a.astype(BF16), b.astype(BF16), (((0,), (0,)), ((), ())),
                           preferred_element_type=F32)


def _split3(x):
    hi = x.astype(BF16)
    r1 = x - hi.astype(F32)
    mid = r1.astype(BF16)
    lo = (r1 - mid.astype(F32)).astype(BF16)
    return hi, mid, lo


def _dot_sel(sel_bf16, x):
    hi, mid, lo = _split3(x)
    d = lambda t: jnp.dot(sel_bf16, t, preferred_element_type=F32)
    return d(hi) + d(mid) + d(lo)


def _dot_sel2_r(x, sel_bf16):
    hi = x.astype(BF16)
    lo = (x - hi.astype(F32)).astype(BF16)
    return (jnp.dot(hi, sel_bf16, preferred_element_type=F32)
            + jnp.dot(lo, sel_bf16, preferred_element_type=F32))


def _rms(x, g):
    return x * lax.rsqrt(jnp.mean(x * x, axis=-1, keepdims=True) + EPS) * g


def _sigmoid(x):
    return 0.5 * jnp.tanh(0.5 * x) + 0.5


def _silu(x, scale=1.0):
    return (0.5 * scale) * x * (jnp.tanh(0.5 * x) + 1.0)


def _tri_incl(n):
    r = lax.broadcasted_iota(jnp.int32, (n, n), 0)
    c = lax.broadcasted_iota(jnp.int32, (n, n), 1)
    return (r >= c).astype(BF16)


def _const_spec(shape):
    return pl.BlockSpec(shape, lambda *_: (0,) * len(shape), pipeline_mode=pl.Buffered(1))


def _slab_specs(shapes, tm):
    d = shapes[0][1]
    assert all(s[0] % tm == 0 for s in shapes)
    if len(shapes) == 1:
        n_tiles = shapes[0][0] // tm
        return [pl.BlockSpec((tm, d), lambda i: (i, 0))], n_tiles, n_tiles
    np_tiles = shapes[0][0] // tm
    specs = [pl.BlockSpec((tm, d), lambda i: (jnp.minimum(i, np_tiles - 1), 0)),
             pl.BlockSpec((tm, d), lambda i: (jnp.maximum(i - np_tiles, 0), 0))]
    return specs, np_tiles, np_tiles + shapes[1][0] // tm


def _slab_tile(x_refs, np_tiles):
    if len(x_refs) == 1:
        return x_refs[0][...]
    return jnp.where(pl.program_id(0) < np_tiles, x_refs[0][...], x_refs[1][...])


def _params(sem="arbitrary"):
    return pltpu.CompilerParams(dimension_semantics=(sem,), vmem_limit_bytes=VMEM_LIMIT)


def _post_ffn_kernel(*refs, n_x, final, np_in, np_out):
    x_refs, (y_ref, wo_ref, bo_ref, gn_ref, w1_ref, w3_ref, w2_ref, gf_ref) = refs[:n_x], refs[n_x:n_x + 8]
    o_refs = refs[n_x + 8:]
    x1 = _slab_tile(x_refs, np_in) + _dot(y_ref[...], wo_ref[...]) + bo_ref[...]
    h = _rms(x1, gn_ref[...]).astype(BF16)
    a = jnp.dot(h, w1_ref[...], preferred_element_type=F32)
    b = jnp.dot(h, w3_ref[...], preferred_element_type=F32)
    out = x1 + _dot(_silu(a) * b, w2_ref[...])
    if final:
        out = _rms(out, gf_ref[...])
    if len(o_refs) == 1:
        o_refs[0][...] = out
    else:
        in_first = pl.program_id(0) < np_out

        @pl.when(in_first)
        def _():
            o_refs[0][...] = out

        @pl.when(jnp.logical_not(in_first))
        def _():
            o_refs[1][...] = out


def _post_ffn(x, y, wo, bo, gn, w1, w3, w2, gf, *, final, out_shapes, tm=512):
    d = x[0].shape[1]
    rows, np_in, n_tiles = _slab_specs([a.shape for a in x], tm)
    out_rows, np_out, _ = _slab_specs(out_shapes, tm)
    return pl.pallas_call(
        functools.partial(_post_ffn_kernel, n_x=len(x), final=final, np_in=np_in, np_out=np_out),
        grid=(n_tiles,),
        in_specs=rows + [pl.BlockSpec((tm, y.shape[1]), lambda i: (i, 0)),
                         _const_spec(wo.shape), _const_spec((1, d)), _const_spec((1, d)),
                         _const_spec(w1.shape), _const_spec(w3.shape), _const_spec(w2.shape),
                         _const_spec((1, d))],
        out_specs=out_rows,
        out_shape=[jax.ShapeDtypeStruct(s, F32) for s in out_shapes],
        compiler_params=_params(),
        name="post_ffn",
    )(*x, y, wo.astype(BF16), bo.reshape(1, d), gn.reshape(1, d), w1.astype(BF16), w3.astype(BF16),
      w2.astype(BF16), gf.reshape(1, d))


def _hgrn_proj_kernel(*refs, n_x, layer, np_tiles):
    x_refs = refs[:n_x]
    (gn_ref, lbp_ref, wq_ref, wf_ref, wi_ref, wg_ref, q_ref, lk_ref, lf_ref, v_ref, gate_ref) = refs[n_x:]
    h = _rms(_slab_tile(x_refs, np_tiles), gn_ref[...]).astype(BF16)
    lbp = lbp_ref[...]
    e = jnp.exp(lbp - jnp.max(lbp, axis=0, keepdims=True))
    lb = jnp.sum(e[:layer + 1], axis=0, keepdims=True) / jnp.sum(e, axis=0, keepdims=True)
    q_ref[...] = _silu(jnp.dot(h, wq_ref[...], preferred_element_type=F32), HG_DK ** -0.5)
    fl = jnp.dot(h, wf_ref[...], preferred_element_type=F32)
    l1p = jnp.log1p(jnp.exp(-jnp.abs(fl)))
    lk_ref[...] = jnp.log(1.0 - lb) - (jnp.maximum(fl, 0.0) + l1p)
    sig = jnp.exp(jnp.minimum(fl, 0.0) - l1p)
    lf_ref[...] = jnp.log(lb + (1.0 - lb) * sig)
    v_ref[...] = jnp.dot(h, wi_ref[...], preferred_element_type=F32)
    gate_ref[...] = _silu(jnp.dot(h, wg_ref[...], preferred_element_type=F32))


def _hgrn_proj(x, gn, lbp, wq, wf, wi, wg, *, layer, tm=512):
    d = x[0].shape[1]
    rows, np_tiles, n_tiles = _slab_specs([a.shape for a in x], tm)
    row = pl.BlockSpec((tm, d), lambda i: (i, 0))
    wspec = _const_spec((d, d))
    out = jax.ShapeDtypeStruct((n_tiles * tm, d), F32)
    return pl.pallas_call(
        functools.partial(_hgrn_proj_kernel, n_x=len(x), layer=layer, np_tiles=np_tiles),
        grid=(n_tiles,),
        in_specs=rows + [_const_spec((1, d)), _const_spec(lbp.shape), wspec, wspec, wspec, wspec],
        out_specs=[row] * 5,
        out_shape=[out] * 5,
        compiler_params=_params("parallel"),
        name="hgrn_proj",
    )(*x, gn.reshape(1, d), lbp, wq.astype(BF16), wf.astype(BF16), wi.astype(BF16), wg.astype(BF16))


def _gla_kernel(q_ref, lk_ref, lf_ref, v_ref, gate_ref, gn_ref, si_ref, og_ref, so_ref,
                st_ref, zp_ref, *, geom, n_heads):
    c = pl.program_id(0)
    _, first, last = _chunk_pos(c, geom)

    @pl.when(first)
    def _():
        for h in range(n_heads):
            st_ref[h] = si_ref[0, h].T

    cum = _dot_sel(_tri_incl(CHUNK), lf_ref[...])
    z_all = lk_ref[...] - cum
    zp_ref[:SUB] = jnp.zeros((SUB, zp_ref.shape[1]), F32)
    zp_ref[SUB:] = z_all
    ones = jnp.ones((HG_DK, LANES), BF16)
    t_idx = lax.broadcasted_iota(jnp.int32, (CHUNK, LANES), 0)
    s_idx = lax.broadcasted_iota(jnp.int32, (CHUNK, LANES), 1)
    pair_delta = jnp.where((s_idx <= t_idx) & (s_idx // SUB == t_idx // SUB), t_idx - s_idx, -1)
    n_sub = CHUNK // SUB
    batch = 4

    heads = range(n_heads)
    sls = [slice(h * HG_DK, (h + 1) * HG_DK) for h in heads]
    q_all, v_all = q_ref[...], v_ref[...]
    q_in = q_all * jnp.exp(cum)
    k_out = jnp.exp(z_all + cum[CHUNK - 1:CHUNK])
    o = [_dot_nt(q_in[:, s], st_ref[h]) for h, s in zip(heads, sls)]
    upd = [_dot_tn(v_all[:, s], k_out[:, s]) for s in sls]
    blocks = [[jnp.zeros((SUB, HG_DK), F32)] for _ in heads]
    for i in range(1, n_sub):
        lo, hi = i * SUB, (i + 1) * SUB
        ref = cum[lo - 1:lo]
        qi = q_all[lo:hi] * jnp.exp(cum[lo:hi] - ref)
        kj = jnp.exp(z_all[:lo] + ref)
        sc = [_dot_nt(qi[:, s], kj[:, s]) for s in sls]
        for h in heads:
            blocks[h].append(_dot(sc[h], v_all[:lo, sls[h]]))
    o = [o[h] + jnp.concatenate(blocks[h], axis=0) for h in heads]
    scores = [jnp.zeros((CHUNK, LANES), F32) for _ in heads]
    for b0 in range(0, SUBLANES, batch):
        deltas = [d for b in range(b0, b0 + batch) for d in (b, b + SUBLANES)]
        sums = []
        for h, s in zip(heads, sls):
            parts = []
            for b in range(b0, b0 + batch):
                zb = zp_ref[SUBLANES - b:SUBLANES - b + CHUNK + SUBLANES, s]
                for zs in (zb[SUBLANES:], zb[:CHUNK]):
                    parts.append((q_all[:, s] * jnp.exp(cum[:, s] + zs)).astype(BF16))
            sums.append(jnp.dot(jnp.concatenate(parts, axis=0), ones, preferred_element_type=F32))
        for h in heads:
            for i, delta in enumerate(deltas):
                scores[h] = jnp.where(pair_delta == delta, sums[h][i * CHUNK:(i + 1) * CHUNK], scores[h])
    o = [o[h] + _dot(scores[h][:, :CHUNK], v_all[:, sls[h]]) for h in heads]
    decay = jnp.exp(cum[CHUNK - 1:CHUNK])
    for h, s in zip(heads, sls):
        st_ref[h] = st_ref[h] * decay[:, s] + upd[h]
        on = o[h] * lax.rsqrt(jnp.mean(o[h] * o[h], axis=-1, keepdims=True) + EPS) * gn_ref[...]
        og_ref[:, s] = on * gate_ref[:, s]

    @pl.when(last)
    def _():
        for h in range(n_heads):
            so_ref[0, h] = st_ref[h].T


def _gla(q, lk, lf, v, gate, gn, s_in, geom: Geom):
    t, d = q.shape
    n_heads = d // HG_DK
    row = pl.BlockSpec((CHUNK, d), lambda c: (c, 0))
    sspec = pl.BlockSpec((1,) + s_in.shape[1:], _seq_index_map(geom, s_in.ndim))
    return pl.pallas_call(
        functools.partial(_gla_kernel, geom=geom, n_heads=n_heads),
        grid=(geom.n_chunks,),
        in_specs=[row] * 5 + [_const_spec((1, HG_DK)), sspec],
        out_specs=[row, sspec],
        out_shape=[jax.ShapeDtypeStruct((t, d), F32), jax.ShapeDtypeStruct(s_in.shape, F32)],
        scratch_shapes=[pltpu.VMEM((n_heads, HG_DK, HG_DK), F32), pltpu.VMEM((CHUNK + SUB, d), F32)],
        compiler_params=_params(),
        name="hgrn_gla",
    )(q, lk, lf, v, gate, gn.reshape(1, HG_DK), s_in)


def _hgrn_layer(x, gn, s_in, p, j, geom):
    q, lk, lf, v, gate = _hgrn_proj(x, gn, p["hg_lb"], p["hg_wq"][j], p["hg_wf"][j],
                                    p["hg_wi"][j], p["hg_wg"][j], layer=j)
    return _gla(q, lk, lf, v, gate, p["hg_gn"][j], s_in, geom)


def _rwkv_proj_kernel(*refs, n_x, geom, tm, np_tiles):
    x_refs, prev_refs = refs[:n_x], refs[n_x:2 * n_x]
    (st_ref, gn_ref, mu_ref, wr_ref, wk_ref, wv_ref, w0_ref, w1_ref, w2_ref, a0_ref, a1_ref, a2_ref,
     g1_ref, g2_ref, kk_ref, ka_ref,
     r_ref, lw_ref, kh_ref, v_ref, kkr_ref, a_ref, g_ref, hl_ref, win_ref) = refs[2 * n_x:]
    i = pl.program_id(0)
    cpt = tm // CHUNK
    gn = gn_ref[...]
    h = _rms(_slab_tile(x_refs, np_tiles), gn)
    win_ref[SUBLANES:] = h
    win_ref[:SUBLANES] = _rms(_slab_tile(prev_refs, np_tiles), gn)
    for j in range(cpt):
        last_row = SUBLANES + (j + 1) * CHUNK - 1
        hl_ref[j] = win_ref[last_row:last_row + 1]
    for j in range(cpt):
        _, first, _ = _chunk_pos(i * cpt + j, geom)

        @pl.when(first)
        def _():
            row = SUBLANES - 1 + j * CHUNK
            win_ref[row:row + 1] = st_ref[j]

    xx = win_ref[SUBLANES - 1:SUBLANES - 1 + tm] - h
    mix = lambda n: (h + xx * mu_ref[n:n + 1]).astype(BF16)
    dot = lambda a, w_ref: jnp.dot(a, w_ref[...], preferred_element_type=F32)
    r_ref[...] = dot(mix(0), wr_ref)
    k = dot(mix(2), wk_ref)
    v_ref[...] = dot(mix(3), wv_ref)
    tw = jnp.tanh(dot(mix(1), w1_ref)).astype(BF16)
    lw_ref[...] = -RW_DECAY_SCALE * _sigmoid(w0_ref[...] + dot(tw, w2_ref))
    a = _sigmoid(a0_ref[...] + dot(dot(mix(4), a1_ref).astype(BF16), a2_ref))
    g_ref[...] = dot(_sigmoid(dot(mix(5), g1_ref)).astype(BF16), g2_ref)
    a_ref[...] = a
    kkr_ref[...] = k * kk_ref[...]
    kh_ref[...] = k * (1.0 + (a - 1.0) * ka_ref[...])


def _rwkv_proj(x, gn, start_rows, p, j, geom, *, tm=256):
    d = x[0].shape[1]
    rows, np_tiles, n_tiles = _slab_specs([a.shape for a in x], tm)
    t = n_tiles * tm
    cpt = tm // CHUNK
    row = pl.BlockSpec((tm, d), lambda i: (i, 0))
    bpt = tm // SUBLANES
    prev = [pl.BlockSpec((SUBLANES, d), lambda i: (jnp.clip(i * bpt - 1, 0, np_tiles * bpt - 1), 0))]
    if len(x) == 2:
        prev.append(pl.BlockSpec((SUBLANES, d), lambda i: (jnp.maximum((i - np_tiles) * bpt - 1, 0), 0)))
    vec = _const_spec((1, d))
    bf = lambda w: w.astype(BF16)
    ws = [bf(p["rw_wr"][j]), bf(p["rw_wk"][j]), bf(p["rw_wv"][j]), p["rw_w0"][j].reshape(1, d),
          bf(p["rw_w1"][j]), bf(p["rw_w2"][j]), p["rw_a0"][j].reshape(1, d), bf(p["rw_a1"][j]),
          bf(p["rw_a2"][j]), bf(p["rw_g1"][j]), bf(p["rw_g2"][j]), p["rw_kk"][j].reshape(1, d),
          p["rw_ka"][j].reshape(1, d)]
    out = jax.ShapeDtypeStruct((t, d), F32)
    return pl.pallas_call(
        functools.partial(_rwkv_proj_kernel, n_x=len(x), geom=geom, tm=tm, np_tiles=np_tiles),
        grid=(n_tiles,),
        in_specs=rows + prev + [pl.BlockSpec((cpt, 1, d), lambda i: (i, 0, 0)), vec,
                                _const_spec((6, d))] + [_const_spec(w.shape) for w in ws],
        out_specs=[row] * 7 + [pl.BlockSpec((cpt, 1, d), lambda i: (i, 0, 0))],
        out_shape=[out] * 7 + [jax.ShapeDtypeStruct((geom.n_chunks, 1, d), F32)],
        scratch_shapes=[pltpu.VMEM((tm + SUBLANES, d), F32)],
        compiler_params=_params(),
        name="rwkv_proj",
    )(*x, *x, start_rows.reshape(geom.n_chunks, 1, d), gn.reshape(1, d), p["rw_mu"][j], *ws)


def _block_diag(x, bd_mask):
    return jnp.where(bd_mask, jnp.concatenate([x, x], axis=0), 0.0)


def _rwkv_kernel(r_ref, lw_ref, kh_ref, v_ref, kkr_ref, a_ref, g_ref, rk_ref, lg_ref, lb_ref, si_ref,
                 y_ref, so_ref, st_ref, *, geom, n_pairs):
    c = pl.program_id(0)
    _, first, last = _chunk_pos(c, geom)
    n = RW_N
    p2 = 2 * n

    @pl.when(first)
    def _():
        z = jnp.zeros((n, n), F32)
        for p in range(n_pairs):
            sa, sb = si_ref[0, 2 * p], si_ref[0, 2 * p + 1]
            st_ref[p] = jnp.concatenate([jnp.concatenate([sa, z], axis=1),
                                         jnp.concatenate([z, sb], axis=1)], axis=0)

    lw = lw_ref[...]
    cw = _dot_sel(_tri_incl(CHUNK), lw)
    w_inc = jnp.exp(cw)
    w_exc = jnp.exp(cw - lw)
    w_inv = jnp.exp(-cw)
    w_end = jnp.exp(cw[CHUNK - 1:CHUNK] - cw)

    ri = lax.broadcasted_iota(jnp.int32, (p2, p2), 0)
    ci = lax.broadcasted_iota(jnp.int32, (p2, p2), 1)
    bd_mask = (ri // n) == (ci // n)
    ones_bd = bd_mask.astype(BF16)
    t_idx = lax.broadcasted_iota(jnp.int32, (CHUNK, p2), 0)
    s_idx = lax.broadcasted_iota(jnp.int32, (CHUNK, p2), 1) % n
    strict = s_idx < t_idx
    incl = s_idx <= t_idx
    eye = (s_idx == t_idx).astype(F32)
    bd = lambda x: _block_diag(x, bd_mask)

    pairs = range(n_pairs)
    sls = [slice(p * p2, (p + 1) * p2) for p in pairs]
    def segsum(xs):
        tot = _dot_sel2_r(jnp.concatenate(xs, axis=0), ones_bd)
        return [tot[i * CHUNK:(i + 1) * CHUNK] for i in range(len(xs))]

    r, kh, v, kkr = r_ref[...], kh_ref[...], v_ref[...], kkr_ref[...]
    rkk = r * kh * rk_ref[...]
    sums = segsum([(kkr * kkr)[:, s] for s in sls] + [rkk[:, s] for s in sls])
    ss = jnp.concatenate(sums[:n_pairs], axis=1)
    bonus = [sums[n_pairs + p] * v[:, sls[p]] for p in pairs]
    kk = kkr / jnp.maximum(jnp.sqrt(ss), 1e-12)
    b = kk * a_ref[...]
    kk_t, r_t = kk * w_exc, r * w_inc
    b_t, k_t = b * w_inv, kh * w_inv
    k_e, b_e = kh * w_end, b * w_end
    st =[st_ref[p] for p in pairs]
    bd_v = [bd(v[:, s]).astype(BF16) for s in sls]
    lhs = [jnp.concatenate([kk_t[:, s], r_t[:, s]], axis=0) for s in sls]
    g_b = [_dot_nt(lhs[p], bd(b_t[:, sls[p]])) for p in pairs]
    g_k = [_dot_nt(lhs[p], bd(k_t[:, sls[p]])) for p in pairs]
    from_state = [_dot_nt(lhs[p], st[p]) for p in pairs]
    rhs = [from_state[p][:CHUNK] + _dot(jnp.where(strict, g_k[p][:CHUNK], 0.0), bd_v[p]) for p in pairs]
    o_kv = [from_state[p][CHUNK:] + _dot(jnp.where(incl, g_k[p][CHUNK:], 0.0), bd_v[p]) for p in pairs]
    npow = [jnp.where(strict, -g_b[p][:CHUNK], 0.0) for p in pairs]
    tinv = [eye + npow[p] for p in pairs]
    npow = [_dot(npow[p], bd(npow[p])) for p in pairs]
    for _ in range(4):
        both = [_dot(jnp.concatenate([npow[p], tinv[p]], axis=0), bd(npow[p])) for p in pairs]
        npow = [both[p][:CHUNK] for p in pairs]
        tinv = [tinv[p] + both[p][CHUNK:] for p in pairs]
    tinv = [tinv[p] + _dot(tinv[p], bd(npow[p])) for p in pairs]
    u = [_dot(tinv[p], bd(rhs[p])) for p in pairs]
    o = [o_kv[p] - _dot(jnp.where(incl, g_b[p][CHUNK:], 0.0), bd(u[p])) for p in pairs]
    for p in pairs:
        s = sls[p]
        upd = _dot_tn(jnp.concatenate([v[:, s], -u[p]], axis=0),
                      jnp.concatenate([k_e[:, s], b_e[:, s]], axis=0))
        st_ref[p] = st[p] * w_inc[CHUNK - 1:CHUNK, s] + jnp.where(bd_mask, upd, 0.0)

    mean = segsum(o)
    oc = [o[p] - mean[p] * (1.0 / n) for p in pairs]
    var = segsum([oc[p] * oc[p] for p in pairs])
    for p in pairs:
        s = sls[p]
        on = oc[p] * lax.rsqrt(var[p] * (1.0 / n) + RW_GN_EPS) * lg_ref[:, s] + lb_ref[:, s]
        y_ref[:, s] = (on + bonus[p]) * g_ref[:, s]

    @pl.when(last)
    def _():
        for p in pairs:
            so_ref[0, 2 * p] = st_ref[p, :n, :n]
            so_ref[0, 2 * p + 1] = st_ref[p, n:, n:]


def _rwkv(r, lw, kh, v, kkr, a, g, rk, lg, lb, s_in, geom: Geom):
    t, d = r.shape
    n_pairs = d // (2 * RW_N)
    row = pl.BlockSpec((CHUNK, d), lambda c: (c, 0))
    vec = _const_spec((1, d))
    sspec = pl.BlockSpec((1,) + s_in.shape[1:], _seq_index_map(geom, s_in.ndim))
    return pl.pallas_call(
        functools.partial(_rwkv_kernel, geom=geom, n_pairs=n_pairs),
        grid=(geom.n_chunks,),
        in_specs=[row] * 7 + [vec] * 3 + [sspec],
        out_specs=[row, sspec],
        out_shape=[jax.ShapeDtypeStruct((t, d), F32), jax.ShapeDtypeStruct(s_in.shape, F32)],
        scratch_shapes=[pltpu.VMEM((n_pairs, 2 * RW_N, 2 * RW_N), F32)],
        compiler_params=_params(),
        name="rwkv_scan",
    )(r, lw, kh, v, kkr, a, g, rk.reshape(1, d), lg.reshape(1, d), lb.reshape(1, d), s_in)


def _rwkv_layer(x, gn, s_in, shift_in, p, j, geom):
    d = x[0].shape[1]
    seq_first = _first_chunks(geom)
    start_rows = jnp.zeros((geom.n_chunks, d), F32).at[seq_first].set(shift_in)
    r, lw, kh, v, kkr, a, g, hl = _rwkv_proj(x, gn, start_rows, p, j, geom)
    y, s_out = _rwkv(r, lw, kh, v, kkr, a, g, p["rw_rk"][j], p["rw_ln_g"][j], p["rw_ln_b"][j], s_in, geom)
    return y, s_out, hl.reshape(geom.n_chunks, d)[_last_chunks(geom)]


def _first_chunks(g: Geom):
    return np.concatenate([np.arange(g.nb_p) * g.cps_p,
                           g.npc + np.arange(g.n_seq - g.nb_p) * g.cps_s])


def _last_chunks(g: Geom):
    return _first_chunks(g) + np.concatenate([np.full(g.nb_p, g.cps_p - 1),
                                              np.full(g.n_seq - g.nb_p, g.cps_s - 1)])


def _gelu_tanh(x):
    return 0.5 * x * (1.0 + jnp.tanh(0.7978845608028654 * (x + 0.044715 * x * x * x)))


def _lru_proj_kernel(*refs, n_x, np_tiles):
    x_refs, (gn_ref, wy_ref, wx_ref, y_ref, ux_ref) = refs[:n_x], refs[n_x:]
    h = _rms(_slab_tile(x_refs, np_tiles), gn_ref[...]).astype(BF16)
    y_ref[...] = _gelu_tanh(jnp.dot(h, wy_ref[...], preferred_element_type=F32))
    ux_ref[...] = jnp.dot(h, wx_ref[...], preferred_element_type=F32)


def _lru_proj(x, gn, wy, wx, *, tm=512):
    d = x[0].shape[1]
    rows, np_tiles, n_tiles = _slab_specs([a.shape for a in x], tm)
    out = jax.ShapeDtypeStruct((n_tiles * tm, wy.shape[1]), F32)
    return pl.pallas_call(
        functools.partial(_lru_proj_kernel, n_x=len(x), np_tiles=np_tiles),
        grid=(n_tiles,),
        in_specs=rows + [_const_spec((1, d)), _const_spec(wy.shape), _const_spec(wx.shape)],
        out_specs=[pl.BlockSpec((tm, wy.shape[1]), lambda i: (i, 0))] * 2,
        out_shape=[out, out],
        compiler_params=_params("parallel"),
        name="lru_proj",
    )(*x, gn.reshape(1, d), wy.astype(BF16), wx.astype(BF16))


def _shift_rows(x, d, fill, row):
    return jnp.where(row < d, fill, pltpu.roll(x, d, 0))


def _lru_kernel(ux_ref, y_ref, cw_ref, cb_ref, gaw_ref, gab_ref, gxw_ref, gxb_ref, lam_ref,
                hi_ref, bi_ref, hy_ref, ho_ref, bo_ref, win_ref, hc_ref, *, geom, conv_k):
    c = pl.program_id(0)
    _, first, last = _chunk_pos(c, geom)
    top = SUBLANES
    tail = conv_k - 1

    @pl.when(first)
    def _():
        win_ref[top - tail:top] = bi_ref[0]
        hc_ref[...] = hi_ref[0]

    win_ref[top:] = ux_ref[...]
    _window_conv(win_ref, cw_ref, cb_ref, hy_ref, top - tail, conv_k)
    u = hy_ref[...]

    @pl.when(last)
    def _():
        bo_ref[0] = win_ref[top + CHUNK - tail:top + CHUNK]

    win_ref[:top] = win_ref[CHUNK:CHUNK + top]

    bw = u.shape[1] // LRU_BLOCKS
    zr, zi = [], []
    for blk in range(LRU_BLOCKS):
        ub = u[:, blk * bw:(blk + 1) * bw].astype(BF16)
        zr.append(jnp.dot(ub, gaw_ref[blk], preferred_element_type=F32))
        zi.append(jnp.dot(ub, gxw_ref[blk], preferred_element_type=F32))
    r = _sigmoid(jnp.concatenate(zr, axis=1) + gab_ref[...])
    i = _sigmoid(jnp.concatenate(zi, axis=1) + gxb_ref[...])
    nl = -lam_ref[...]
    softplus = jnp.maximum(nl, 0.0) + jnp.log1p(jnp.exp(-jnp.abs(nl)))
    log_a = -LRU_C * r * softplus
    a = jnp.exp(log_a)
    bterm = jnp.sqrt(-jnp.tanh(log_a) * (a * a + 1.0)) * (i * u)

    row = lax.broadcasted_iota(jnp.int32, a.shape, 0)
    d = 1
    while d < CHUNK:
        a_s = _shift_rows(a, d, 1.0, row)
        b_s = _shift_rows(bterm, d, 0.0, row)
        bterm = bterm + a * b_s
        a = a * a_s
        d *= 2
    h = a * hc_ref[...] + bterm
    hy_ref[...] = h * y_ref[...]
    hc_ref[...] = h[CHUNK - 1:CHUNK]

    @pl.when(last)
    def _():
        ho_ref[0] = h[CHUNK - 1:CHUNK]


def _lru(ux, y, h_in, buf_in, p, j, geom: Geom):
    t, w = ux.shape
    conv_k = p["lru_conv_w"].shape[1]
    row = pl.BlockSpec((CHUNK, w), lambda c: (c, 0))
    vec = _const_spec((1, w))
    gspec = _const_spec(p["lru_ga_w"].shape[1:])
    hspec = pl.BlockSpec((1, 1, w), _seq_index_map(geom, 3))
    bspec = pl.BlockSpec((1, conv_k - 1, w), _seq_index_map(geom, 3))
    h3 = h_in.reshape(geom.n_seq, 1, w)
    return pl.pallas_call(
        functools.partial(_lru_kernel, geom=geom, conv_k=conv_k),
        grid=(geom.n_chunks,),
        in_specs=[row, row, _const_spec((conv_k, w)), vec, gspec, vec, gspec, vec, vec, hspec, bspec],
        out_specs=[row, hspec, bspec],
        out_shape=[jax.ShapeDtypeStruct((t, w), F32), jax.ShapeDtypeStruct(h3.shape, F32),
                   jax.ShapeDtypeStruct(buf_in.shape, F32)],
        scratch_shapes=[pltpu.VMEM((CHUNK + SUBLANES, w), F32), pltpu.VMEM((1, w), F32)],
        compiler_params=_params(),
        name="lru_scan",
    )(ux, y, p["lru_conv_w"][j], p["lru_conv_b"][j].reshape(1, w), p["lru_ga_w"][j].astype(BF16),
      p["lru_ga_b"][j].reshape(1, w), p["lru_gx_w"][j].astype(BF16), p["lru_gx_b"][j].reshape(1, w),
      p["lru_lam"][j].reshape(1, w), h3, buf_in)


def _lru_layer(x, gn, h_in, buf_in, p, j, geom):
    y, ux = _lru_proj(x, gn, p["lru_wy"][j], p["lru_wx"][j])
    hy, h_out, buf_out = _lru(ux, y, h_in, buf_in, p, j, geom)
    return hy, h_out.reshape(h_in.shape), buf_out


def _conf_proj_kernel(*refs, n_x, np_tiles):
    x_refs, (gn_ref, w1_ref, b1_ref, u_ref) = refs[:n_x], refs[n_x:]
    d = u_ref.shape[1]
    h = _rms(_slab_tile(x_refs, np_tiles), gn_ref[...]).astype(BF16)
    hh = jnp.dot(h, w1_ref[...], preferred_element_type=F32) + b1_ref[...]
    u_ref[...] = hh[:, :d] * _sigmoid(hh[:, d:])


def _conf_proj(x, gn, w1, b1, *, tm=512):
    d = x[0].shape[1]
    rows, np_tiles, n_tiles = _slab_specs([a.shape for a in x], tm)
    return pl.pallas_call(
        functools.partial(_conf_proj_kernel, n_x=len(x), np_tiles=np_tiles),
        grid=(n_tiles,),
        in_specs=rows + [_const_spec((1, d)), _const_spec(w1.shape), _const_spec((1, w1.shape[1]))],
        out_specs=pl.BlockSpec((tm, d), lambda i: (i, 0)),
        out_shape=jax.ShapeDtypeStruct((n_tiles * tm, d), F32),
        compiler_params=_params("parallel"),
        name="conf_proj",
    )(*x, gn.reshape(1, d), w1.astype(BF16), b1.reshape(1, -1))


def _window_conv(win_ref, w_ref, b_ref, out_ref, base, conv_k):
    for lt in range(out_ref.shape[1] // LANES):
        sl = slice(lt * LANES, (lt + 1) * LANES)
        acc = jnp.broadcast_to(b_ref[:, sl], (CHUNK, LANES))
        for b in range(SUBLANES):
            offs = [o for o in range(base, base + conv_k) if o % SUBLANES == b]
            if not offs:
                continue
            rows = CHUNK + SUBLANES if b else CHUNK
            part = sum(w_ref[o - base:o - base + 1, sl] * win_ref[o - b:o - b + rows, sl] for o in offs)
            acc = acc + (pltpu.roll(part, rows - b, 0)[:CHUNK] if b else part)
        out_ref[:, sl] = acc


def _conf_kernel(u_ref, w_ref, b_ref, lg_ref, lb_ref, bi_ref, c_ref, bo_ref, win_ref, *, geom, conv_k, top):
    c = pl.program_id(0)
    _, first, last = _chunk_pos(c, geom)
    tail = conv_k - 1

    @pl.when(first)
    def _():
        win_ref[top - tail:top] = bi_ref[0]

    win_ref[top:] = u_ref[...]
    _window_conv(win_ref, w_ref, b_ref, c_ref, top - tail, conv_k)
    acc = c_ref[...]

    @pl.when(last)
    def _():
        bo_ref[0] = win_ref[top + CHUNK - tail:top + CHUNK]

    win_ref[:top] = win_ref[CHUNK:CHUNK + top]
    mu = jnp.mean(acc, axis=-1, keepdims=True)
    xc = acc - mu
    var = jnp.mean(xc * xc, axis=-1, keepdims=True)
    c_ref[...] = _silu(xc * lax.rsqrt(var + EPS) * lg_ref[...] + lb_ref[...])


def _conf(u, buf_in, p, j, geom: Geom):
    t, d = u.shape
    conv_k = p["cf_dw_w"].shape[1]
    top = -(-(conv_k - 1) // SUBLANES) * SUBLANES
    assert top <= CHUNK
    row = pl.BlockSpec((CHUNK, d), lambda c: (c, 0))
    vec = _const_spec((1, d))
    bspec = pl.BlockSpec((1, conv_k - 1, d), _seq_index_map(geom, 3))
    return pl.pallas_call(
        functools.partial(_conf_kernel, geom=geom, conv_k=conv_k, top=top),
        grid=(geom.n_chunks,),
        in_specs=[row, _const_spec((conv_k, d)), vec, vec, vec, bspec],
        out_specs=[row, bspec],
        out_shape=[jax.ShapeDtypeStruct((t, d), F32), jax.ShapeDtypeStruct(buf_in.shape, F32)],
        scratch_shapes=[pltpu.VMEM((CHUNK + top, d), F32)],
        compiler_params=_params(),
        name="conf_conv",
    )(u, p["cf_dw_w"][j], p["cf_dw_b"][j].reshape(1, d), p["cf_ln_g"][j].reshape(1, d),
      p["cf_ln_b"][j].reshape(1, d), buf_in)


def _conf_layer(x, gn, buf_in, p, j, geom):
    u = _conf_proj(x, gn, p["cf_w1"][j], p["cf_b1"][j])
    return _conf(u, buf_in, p, j, geom)


_ARG_NAMES = (
    "x_prompt x_sample state_hgrn state_rwkv state_rwkv_shift state_lru state_lru_conv state_conf_conv "
    "norm_mix norm_ffn norm_final hg_wq hg_wf hg_wi hg_wg hg_gn hg_wo hg_lb rw_mu rw_wr rw_wk rw_wv rw_w0 "
    "rw_w1 rw_w2 rw_a0 rw_a1 rw_a2 rw_g1 rw_g2 rw_kk rw_ka rw_rk rw_ln_g rw_ln_b rw_wo lru_wy lru_wx "
    "lru_conv_w lru_conv_b lru_ga_w lru_ga_b lru_gx_w lru_gx_b lru_lam lru_wo cf_w1 cf_b1 cf_dw_w cf_dw_b "
    "cf_ln_g cf_ln_b cf_w2 cf_b2 ffn_w1 ffn_w3 ffn_w2").split()
N_MIXERS = 4


def kernel(x_prompt, x_sample, state_hgrn, state_rwkv, state_rwkv_shift, state_lru, state_lru_conv, state_conf_conv, norm_mix, norm_ffn, norm_final, hg_wq, hg_wf, hg_wi, hg_wg, hg_gn, hg_wo, hg_lb, rw_mu, rw_wr, rw_wk, rw_wv, rw_w0, rw_w1, rw_w2, rw_a0, rw_a1, rw_a2, rw_g1, rw_g2, rw_kk, rw_ka, rw_rk, rw_ln_g, rw_ln_b, rw_wo, lru_wy, lru_wx, lru_conv_w, lru_conv_b, lru_ga_w, lru_ga_b, lru_gx_w, lru_gx_b, lru_lam, lru_wo, cf_w1, cf_b1, cf_dw_w, cf_dw_b, cf_ln_g, cf_ln_b, cf_w2, cf_b2, ffn_w1, ffn_w3, ffn_w2):
    p = dict(zip(_ARG_NAMES, (x_prompt, x_sample, state_hgrn, state_rwkv, state_rwkv_shift, state_lru, state_lru_conv, state_conf_conv, norm_mix, norm_ffn, norm_final, hg_wq, hg_wf, hg_wi, hg_wg, hg_gn, hg_wo, hg_lb, rw_mu, rw_wr, rw_wk, rw_wv, rw_w0, rw_w1, rw_w2, rw_a0, rw_a1, rw_a2, rw_g1, rw_g2, rw_kk, rw_ka, rw_rk, rw_ln_g, rw_ln_b, rw_wo, lru_wy, lru_wx, lru_conv_w, lru_conv_b, lru_ga_w, lru_ga_b, lru_gx_w, lru_gx_b, lru_lam, lru_wo, cf_w1, cf_b1, cf_dw_w, cf_dw_b, cf_ln_g, cf_ln_b, cf_w2, cf_b2, ffn_w1, ffn_w3, ffn_w2)))
    nb_p, seq_p, d = x_prompt.shape
    nb_s, seq_s, _ = x_sample.shape
    assert seq_p % CHUNK == 0 and seq_s % CHUNK == 0
    npc = nb_p * seq_p // CHUNK
    geom = Geom(n_chunks=npc + nb_s * seq_s // CHUNK, npc=npc, cps_p=seq_p // CHUNK,
                cps_s=seq_s // CHUNK, nb_p=nb_p, n_seq=nb_p + nb_s)
    x = (x_prompt.reshape(-1, d), x_sample.reshape(-1, d))
    split_shapes = [a.shape for a in x]
    n_rows = sum(s[0] for s in split_shapes)

    def all_seqs(state):
        return jnp.concatenate([jnp.zeros((nb_p,) + state.shape[1:], state.dtype), state], axis=0)

    depth = norm_mix.shape[0]
    zero_bias = jnp.zeros((d,), F32)
    outs = {k: [] for k in ("hg", "rw", "sh", "lh", "lc", "cf")}
    for i in range(depth):
        m, j = i % N_MIXERS, i // N_MIXERS
        gn = norm_mix[i]
        bo = zero_bias
        if m == 0:
            y, s = _hgrn_layer(x, gn, all_seqs(state_hgrn[j]), p, j, geom)
            outs["hg"].append(s)
            wo = hg_wo[j]
        elif m == 1:
            y, s, sh = _rwkv_layer(x, gn, all_seqs(state_rwkv[j]), all_seqs(state_rwkv_shift[j]), p, j, geom)
            outs["rw"].append(s)
            outs["sh"].append(sh)
            wo = rw_wo[j]
        elif m == 2:
            y, hl, cb = _lru_layer(x, gn, all_seqs(state_lru[j]), all_seqs(state_lru_conv[j]), p, j, geom)
            outs["lh"].append(hl)
            outs["lc"].append(cb)
            wo = lru_wo[j]
        else:
            y, cb = _conf_layer(x, gn, all_seqs(state_conf_conv[j]), p, j, geom)
            outs["cf"].append(cb)
            wo, bo = cf_w2[j], cf_b2[j]
        final = i == depth - 1
        out_shapes = split_shapes if final else [(n_rows, d)]
        x = _post_ffn(x, y, wo, bo, norm_ffn[i], ffn_w1[i], ffn_w3[i], ffn_w2[i], norm_final,
                      final=final, out_shapes=out_shapes)

    stacked = [jnp.stack(outs[k]) for k in ("hg", "rw", "sh", "lh", "lc", "cf")]
    return ((x[0].reshape(nb_p, seq_p, d), x[1].reshape(nb_s, seq_s, d))
            + tuple(s[:, :nb_p] for s in stacked) + tuple(s[:, nb_p:] for s in stacked))
```

```python
import functools
from typing import NamedTuple

import numpy as np
import jax
import jax.numpy as jnp
from jax import lax
from jax.experimental import pallas as pl
from jax.experimental.pallas import tpu as pltpu

F32 = jnp.float32
BF16 = jnp.bfloat16

EPS = 1e-6
CHUNK = 64
SUB = 16
LANES = 128
SUBLANES = 8
HG_DK = 128
RW_N = 64
RW_CB = 2
HG_CB = 2
RW_DECAY_SCALE = 0.6065306597126334
RW_GN_EPS = 64e-5
LRU_BLOCKS = 8
LRU_C = 8.0
VMEM_LIMIT = 56 * 1024 * 1024


class Geom(NamedTuple):
    n_chunks: int
    npc: int
    cps_p: int
    cps_s: int
    nb_p: int
    n_seq: int


def _chunk_pos(c, g: Geom):
    in_p = c < g.npc
    cs = c - g.npc
    seq = jnp.where(in_p, c // g.cps_p, g.nb_p + cs // g.cps_s)
    first = jnp.where(in_p, c % g.cps_p == 0, cs % g.cps_s == 0)
    last = jnp.where(in_p, c % g.cps_p == g.cps_p - 1, cs % g.cps_s == g.cps_s - 1)
    return seq, first, last


def _seq_index_map(g: Geom, ndim):
    def index_map(c):
        seq, _, _ = _chunk_pos(c, g)
        return (seq,) + (0,) * (ndim - 1)
    return index_map


def _dot(a, b):
    return jnp.dot(a.astype(BF16), b.astype(BF16), preferred_element_type=F32)


def _dot_nt(a, b):
    return lax.dot_general(a.astype(BF16), b.astype(BF16), (((1,), (1,)), ((), ())),
                           preferred_element_type=F32)


def _dot_tn(a, b):
    return lax.dot_general(a.astype(BF16), b.astype(BF16), (((0,), (0,)), ((), ())),
                           preferred_element_type=F32)


def _split3(x):
    hi = x.astype(BF16)
    r1 = x - hi.astype(F32)
    mid = r1.astype(BF16)
    lo = (r1 - mid.astype(F32)).astype(BF16)
    return hi, mid, lo


def _dot_sel(sel_bf16, x):
    hi, mid, lo = _split3(x)
    d = lambda t: jnp.dot(sel_bf16, t, preferred_element_type=F32)
    return d(hi) + d(mid) + d(lo)


def _dot_sel2_r(x, sel_bf16):
    hi = x.astype(BF16)
    lo = (x - hi.astype(F32)).astype(BF16)
    return (jnp.dot(hi, sel_bf16, preferred_element_type=F32)
            + jnp.dot(lo, sel_bf16, preferred_element_type=F32))


def _rms(x, g):
    return x * lax.rsqrt(jnp.mean(x * x, axis=-1, keepdims=True) + EPS) * g


def _sigmoid(x):
    return 0.5 * jnp.tanh(0.5 * x) + 0.5


def _silu(x, scale=1.0):
    return (0.5 * scale) * x * (jnp.tanh(0.5 * x) + 1.0)


def _tri_incl(n):
    r = lax.broadcasted_iota(jnp.int32, (n, n), 0)
    c = lax.broadcasted_iota(jnp.int32, (n, n), 1)
    return (r >= c).astype(BF16)


def _const_spec(shape):
    return pl.BlockSpec(shape, lambda *_: (0,) * len(shape), pipeline_mode=pl.Buffered(1))


def _slab_specs(shapes, tm):
    d = shapes[0][1]
    assert all(s[0] % tm == 0 for s in shapes)
    if len(shapes) == 1:
        n_tiles = shapes[0][0] // tm
        return [pl.BlockSpec((tm, d), lambda i: (i, 0))], n_tiles, n_tiles
    np_tiles = shapes[0][0] // tm
    specs = [pl.BlockSpec((tm, d), lambda i: (jnp.minimum(i, np_tiles - 1), 0)),
             pl.BlockSpec((tm, d), lambda i: (jnp.maximum(i - np_tiles, 0), 0))]
    return specs, np_tiles, np_tiles + shapes[1][0] // tm


def _slab_tile(x_refs, np_tiles):
    if len(x_refs) == 1:
        return x_refs[0][...]
    return jnp.where(pl.program_id(0) < np_tiles, x_refs[0][...], x_refs[1][...])


def _params(sem="arbitrary"):
    return pltpu.CompilerParams(dimension_semantics=(sem,), vmem_limit_bytes=VMEM_LIMIT)


def _post_ffn_kernel(*refs, n_x, final, np_in, np_out):
    x_refs, (y_ref, wo_ref, bo_ref, gn_ref, w1_ref, w3_ref, w2_ref, gf_ref) = refs[:n_x], refs[n_x:n_x + 8]
    o_refs = refs[n_x + 8:]
    x1 = _slab_tile(x_refs, np_in) + _dot(y_ref[...], wo_ref[...]) + bo_ref[...]
    h = _rms(x1, gn_ref[...]).astype(BF16)
    a = jnp.dot(h, w1_ref[...], preferred_element_type=F32)
    b = jnp.dot(h, w3_ref[...], preferred_element_type=F32)
    out = x1 + _dot(_silu(a) * b, w2_ref[...])
    if final:
        out = _rms(out, gf_ref[...])
    if len(o_refs) == 1:
        o_refs[0][...] = out
    else:
        in_first = pl.program_id(0) < np_out

        @pl.when(in_first)
        def _():
            o_refs[0][...] = out

        @pl.when(jnp.logical_not(in_first))
        def _():
            o_refs[1][...] = out


def _post_ffn(x, y, wo, bo, gn, w1, w3, w2, gf, *, final, out_shapes, tm=512):
    d = x[0].shape[1]
    rows, np_in, n_tiles = _slab_specs([a.shape for a in x], tm)
    out_rows, np_out, _ = _slab_specs(out_shapes, tm)
    return pl.pallas_call(
        functools.partial(_post_ffn_kernel, n_x=len(x), final=final, np_in=np_in, np_out=np_out),
        grid=(n_tiles,),
        in_specs=rows + [pl.BlockSpec((tm, y.shape[1]), lambda i: (i, 0)),
                         _const_spec(wo.shape), _const_spec((1, d)), _const_spec((1, d)),
                         _const_spec(w1.shape), _const_spec(w3.shape), _const_spec(w2.shape),
                         _const_spec((1, d))],
        out_specs=out_rows,
        out_shape=[jax.ShapeDtypeStruct(s, F32) for s in out_shapes],
        compiler_params=_params(),
        name="post_ffn",
    )(*x, y, wo.astype(BF16), bo.reshape(1, d), gn.reshape(1, d), w1.astype(BF16), w3.astype(BF16),
      w2.astype(BF16), gf.reshape(1, d))


def _hgrn_proj_kernel(*refs, n_x, layer, np_tiles):
    x_refs = refs[:n_x]
    (gn_ref, lbp_ref, wq_ref, wf_ref, wi_ref, wg_ref, q_ref, lk_ref, lf_ref, v_ref, gate_ref) = refs[n_x:]
    h = _rms(_slab_tile(x_refs, np_tiles), gn_ref[...]).astype(BF16)
    lbp = lbp_ref[...]
    e = jnp.exp(lbp - jnp.max(lbp, axis=0, keepdims=True))
    lb = jnp.sum(e[:layer + 1], axis=0, keepdims=True) / jnp.sum(e, axis=0, keepdims=True)
    q_ref[...] = _silu(jnp.dot(h, wq_ref[...], preferred_element_type=F32), HG_DK ** -0.5)
    fl = jnp.dot(h, wf_ref[...], preferred_element_type=F32)
    l1p = jnp.log1p(jnp.exp(-jnp.abs(fl)))
    lk_ref[...] = jnp.log(1.0 - lb) - (jnp.maximum(fl, 0.0) + l1p)
    sig = jnp.exp(jnp.minimum(fl, 0.0) - l1p)
    lf_ref[...] = jnp.log(lb + (1.0 - lb) * sig)
    v_ref[...] = jnp.dot(h, wi_ref[...], preferred_element_type=F32)
    gate_ref[...] = _silu(jnp.dot(h, wg_ref[...], preferred_element_type=F32))


def _hgrn_proj(x, gn, lbp, wq, wf, wi, wg, *, layer, tm=512):
    d = x[0].shape[1]
    rows, np_tiles, n_tiles = _slab_specs([a.shape for a in x], tm)
    row = pl.BlockSpec((tm, d), lambda i: (i, 0))
    wspec = _const_spec((d, d))
    out = jax.ShapeDtypeStruct((n_tiles * tm, d), F32)
    return pl.pallas_call(
        functools.partial(_hgrn_proj_kernel, n_x=len(x), layer=layer, np_tiles=np_tiles),
        grid=(n_tiles,),
        in_specs=rows + [_const_spec((1, d)), _const_spec(lbp.shape), wspec, wspec, wspec, wspec],
        out_specs=[row] * 5,
        out_shape=[out] * 5,
        compiler_params=_params("parallel"),
        name="hgrn_proj",
    )(*x, gn.reshape(1, d), lbp, wq.astype(BF16), wf.astype(BF16), wi.astype(BF16), wg.astype(BF16))


def _gla_kernel(q_ref, lk_ref, lf_ref, v_ref, gate_ref, gn_ref, si_ref, og_ref, so_ref,
                st_ref, zp_ref, *, geom, n_heads):
    step = pl.program_id(0)
    chunks = range(HG_CB)
    rows = [slice(j * CHUNK, (j + 1) * CHUNK) for j in chunks]

    @pl.when(step == 0)
    def _():
        st_ref[...] = jnp.zeros(st_ref.shape, F32)

    tri = _tri_incl(CHUNK)
    lf_all = lf_ref[...]
    cum = jnp.concatenate([_dot_sel(tri, lf_all[rj]) for rj in rows], axis=0)
    z_all = lk_ref[...] - cum
    for j in chunks:
        zp_ref[j, :SUB] = jnp.zeros((SUB, zp_ref.shape[2]), F32)
        zp_ref[j, SUB:] = z_all[rows[j]]
    ones = jnp.ones((HG_DK, LANES), BF16)
    t_idx = lax.broadcasted_iota(jnp.int32, (CHUNK, LANES), 0)
    s_idx = lax.broadcasted_iota(jnp.int32, (CHUNK, LANES), 1)
    pair_delta = jnp.where((s_idx <= t_idx) & (s_idx // SUB == t_idx // SUB), t_idx - s_idx, -1)
    n_sub = CHUNK // SUB
    batch = 4

    heads = range(n_heads)
    sls = [slice(h * HG_DK, (h + 1) * HG_DK) for h in heads]
    units = [(j, h) for j in chunks for h in heads]
    tile = lambda x, u: x[rows[u[0]], sls[u[1]]]
    q_all, v_all = q_ref[...], v_ref[...]
    last_cum = jnp.concatenate([jnp.broadcast_to(cum[rj][CHUNK - 1:CHUNK], (CHUNK, cum.shape[1])) for rj in rows],
                               axis=0)
    q_in = q_all * jnp.exp(cum)
    k_out = jnp.exp(z_all + last_cum)
    upd = [_dot_tn(tile(v_all, u), tile(k_out, u)) for u in units]
    blocks = [[jnp.zeros((SUB, HG_DK), F32)] for _ in units]
    for i in range(1, n_sub):
        lo, hi = i * SUB, (i + 1) * SUB
        sc = []
        for j, h in units:
            cj, zj, qj = cum[rows[j]][:, sls[h]], z_all[rows[j]][:, sls[h]], q_all[rows[j]][:, sls[h]]
            ref = cj[lo - 1:lo]
            sc.append(_dot_nt(qj[lo:hi] * jnp.exp(cj[lo:hi] - ref), jnp.exp(zj[:lo] + ref)))
        for i_u, u in enumerate(units):
            blocks[i_u].append(_dot(sc[i_u], tile(v_all, u)[:lo]))
    scores = [jnp.zeros((CHUNK, LANES), F32) for _ in units]
    for b0 in range(0, SUBLANES, batch):
        deltas = [d for b in range(b0, b0 + batch) for d in (b, b + SUBLANES)]
        sums = []
        for j, h in units:
            parts = []
            for b in range(b0, b0 + batch):
                zb = zp_ref[j, SUBLANES - b:SUBLANES - b + CHUNK + SUBLANES, sls[h]]
                for zs in (zb[SUBLANES:], zb[:CHUNK]):
                    parts.append((tile(q_all, (j, h)) * jnp.exp(tile(cum, (j, h)) + zs)).astype(BF16))
            sums.append(jnp.dot(jnp.concatenate(parts, axis=0), ones, preferred_element_type=F32))
        for i_u in range(len(units)):
            for i, delta in enumerate(deltas):
                scores[i_u] = jnp.where(pair_delta == delta, sums[i_u][i * CHUNK:(i + 1) * CHUNK], scores[i_u])
    o_local = [jnp.concatenate(blocks[i_u], axis=0) + _dot(scores[i_u][:, :CHUNK], tile(v_all, u))
               for i_u, u in enumerate(units)]

    state = [st_ref[h] for h in heads]
    decay = jnp.exp(last_cum)
    for j in chunks:
        seq, first, _ = _chunk_pos(step * HG_CB + j, geom)
        for h in heads:
            st0 = jnp.where(first, si_ref[seq, h], state[h])
            o = o_local[j * n_heads + h] + _dot_nt(tile(q_in, (j, h)), st0)
            state[h] = st0 * tile(decay, (j, h))[:1] + upd[j * n_heads + h]
            so_ref[seq, h] = state[h]
            on = o * lax.rsqrt(jnp.mean(o * o, axis=-1, keepdims=True) + EPS) * gn_ref[...]
            og_ref[rows[j], sls[h]] = on * gate_ref[rows[j], sls[h]]
    for h in heads:
        st_ref[h] = state[h]


def _gla(q, lk, lf, v, gate, gn, s_in, geom: Geom):
    t, d = q.shape
    n_heads = d // HG_DK
    assert geom.n_chunks % HG_CB == 0
    rb = HG_CB * CHUNK
    row = pl.BlockSpec((rb, d), lambda c: (c, 0))
    s_t = jnp.swapaxes(s_in, -1, -2)
    og, so = pl.pallas_call(
        functools.partial(_gla_kernel, geom=geom, n_heads=n_heads),
        grid=(geom.n_chunks // HG_CB,),
        in_specs=[row] * 5 + [_const_spec((1, HG_DK)), _const_spec(s_t.shape)],
        out_specs=[row, _const_spec(s_t.shape)],
        out_shape=[jax.ShapeDtypeStruct((t, d), F32), jax.ShapeDtypeStruct(s_t.shape, F32)],
        scratch_shapes=[pltpu.VMEM((n_heads, HG_DK, HG_DK), F32), pltpu.VMEM((HG_CB, CHUNK + SUB, d), F32)],
        compiler_params=_params(),
        name="hgrn_gla",
    )(q, lk, lf, v, gate, gn.reshape(1, HG_DK), s_t)
    return og, jnp.swapaxes(so, -1, -2)


def _hgrn_layer(x, gn, s_in, p, j, geom):
    q, lk, lf, v, gate = _hgrn_proj(x, gn, p["hg_lb"], p["hg_wq"][j], p["hg_wf"][j],
                                    p["hg_wi"][j], p["hg_wg"][j], layer=j)
    return _gla(q, lk, lf, v, gate, p["hg_gn"][j], s_in, geom)


def _rwkv_proj_kernel(*refs, n_x, geom, tm, np_tiles):
    x_refs, prev_refs = refs[:n_x], refs[n_x:2 * n_x]
    (st_ref, gn_ref, mu_ref, wr_ref, wk_ref, wv_ref, w0_ref, w1_ref, w2_ref, a0_ref, a1_ref, a2_ref,
     g1_ref, g2_ref, kk_ref, ka_ref,
     r_ref, lw_ref, kh_ref, v_ref, kkr_ref, a_ref, g_ref, hl_ref, win_ref) = refs[2 * n_x:]
    i = pl.program_id(0)
    cpt = tm // CHUNK
    gn = gn_ref[...]
    h = _rms(_slab_tile(x_refs, np_tiles), gn)
    win_ref[SUBLANES:] = h
    win_ref[:SUBLANES] = _rms(_slab_tile(prev_refs, np_tiles), gn)
    for j in range(cpt):
        last_row = SUBLANES + (j + 1) * CHUNK - 1
        hl_ref[j] = win_ref[last_row:last_row + 1]
    for j in range(cpt):
        _, first, _ = _chunk_pos(i * cpt + j, geom)

        @pl.when(first)
        def _():
            row = SUBLANES - 1 + j * CHUNK
            win_ref[row:row + 1] = st_ref[j]

    xx = win_ref[SUBLANES - 1:SUBLANES - 1 + tm] - h
    mix = lambda n: (h + xx * mu_ref[n:n + 1]).astype(BF16)
    dot = lambda a, w_ref: jnp.dot(a, w_ref[...], preferred_element_type=F32)
    r_ref[...] = dot(mix(0), wr_ref)
    k = dot(mix(2), wk_ref)
    v_ref[...] = dot(mix(3), wv_ref)
    tw = jnp.tanh(dot(mix(1), w1_ref)).astype(BF16)
    lw_ref[...] = -RW_DECAY_SCALE * _sigmoid(w0_ref[...] + dot(tw, w2_ref))
    a = _sigmoid(a0_ref[...] + dot(dot(mix(4), a1_ref).astype(BF16), a2_ref))
    g_ref[...] = dot(_sigmoid(dot(mix(5), g1_ref)).astype(BF16), g2_ref)
    a_ref[...] = a
    kkr_ref[...] = k * kk_ref[...]
    kh_ref[...] = k * (1.0 + (a - 1.0) * ka_ref[...])


def _rwkv_proj(x, gn, start_rows, p, j, geom, *, tm=256):
    d = x[0].shape[1]
    rows, np_tiles, n_tiles = _slab_specs([a.shape for a in x], tm)
    t = n_tiles * tm
    cpt = tm // CHUNK
    row = pl.BlockSpec((tm, d), lambda i: (i, 0))
    bpt = tm // SUBLANES
    prev = [pl.BlockSpec((SUBLANES, d), lambda i: (jnp.clip(i * bpt - 1, 0, np_tiles * bpt - 1), 0))]
    if len(x) == 2:
        prev.append(pl.BlockSpec((SUBLANES, d), lambda i: (jnp.maximum((i - np_tiles) * bpt - 1, 0), 0)))
    vec = _const_spec((1, d))
    bf = lambda w: w.astype(BF16)
    ws = [bf(p["rw_wr"][j]), bf(p["rw_wk"][j]), bf(p["rw_wv"][j]), p["rw_w0"][j].reshape(1, d),
          bf(p["rw_w1"][j]), bf(p["rw_w2"][j]), p["rw_a0"][j].reshape(1, d), bf(p["rw_a1"][j]),
          bf(p["rw_a2"][j]), bf(p["rw_g1"][j]), bf(p["rw_g2"][j]), p["rw_kk"][j].reshape(1, d),
          p["rw_ka"][j].reshape(1, d)]
    out = jax.ShapeDtypeStruct((t, d), F32)
    return pl.pallas_call(
        functools.partial(_rwkv_proj_kernel, n_x=len(x), geom=geom, tm=tm, np_tiles=np_tiles),
        grid=(n_tiles,),
        in_specs=rows + prev + [pl.BlockSpec((cpt, 1, d), lambda i: (i, 0, 0)), vec,
                                _const_spec((6, d))] + [_const_spec(w.shape) for w in ws],
        out_specs=[row] * 7 + [pl.BlockSpec((cpt, 1, d), lambda i: (i, 0, 0))],
        out_shape=[out] * 7 + [jax.ShapeDtypeStruct((geom.n_chunks, 1, d), F32)],
        scratch_shapes=[pltpu.VMEM((tm + SUBLANES, d), F32)],
        compiler_params=_params(),
        name="rwkv_proj",
    )(*x, *x, start_rows.reshape(geom.n_chunks, 1, d), gn.reshape(1, d), p["rw_mu"][j], *ws)


def _block_diag(x, bd_mask):
    return jnp.where(bd_mask, jnp.concatenate([x, x], axis=0), 0.0)


def _rwkv_kernel(r_ref, lw_ref, kh_ref, v_ref, kkr_ref, a_ref, g_ref, rk_ref, lg_ref, lb_ref, si_ref,
                 y_ref, so_ref, st_ref, lhs_s, tinv_s, brb_s, keb_s, vb_s, rhs0_s, okv_s, bonus_s, decay_s,
                 *, geom, n_pairs):
    step = pl.program_id(0)
    wslot = step % 2
    rslot = 1 - wslot
    n = RW_N
    p2 = 2 * n

    @pl.when(step == 0)
    def _():
        for ref in (lhs_s, tinv_s, brb_s, keb_s, vb_s, rhs0_s, okv_s, bonus_s, decay_s):
            ref[1] = jnp.zeros(ref.shape[1:], ref.dtype)
        st_ref[...] = jnp.zeros(st_ref.shape, F32)

    chunks = range(RW_CB)
    rows = [slice(j * CHUNK, (j + 1) * CHUNK) for j in chunks]
    lw = lw_ref[...]
    tri = _tri_incl(CHUNK)
    cw = jnp.concatenate([_dot_sel(tri, lw[rj]) for rj in rows], axis=0)
    w_inc = jnp.exp(cw)
    w_exc = jnp.exp(cw - lw)
    w_inv = jnp.exp(-cw)
    w_end = jnp.exp(jnp.concatenate([cw[rj][CHUNK - 1:CHUNK] - cw[rj] for rj in rows], axis=0))

    ri = lax.broadcasted_iota(jnp.int32, (p2, p2), 0)
    ci = lax.broadcasted_iota(jnp.int32, (p2, p2), 1)
    bd_mask = (ri // n) == (ci // n)
    ones_bd = bd_mask.astype(BF16)
    t_idx = lax.broadcasted_iota(jnp.int32, (CHUNK, p2), 0)
    s_idx = lax.broadcasted_iota(jnp.int32, (CHUNK, p2), 1) % n
    strict = s_idx < t_idx
    incl = s_idx <= t_idx
    eye = (s_idx == t_idx).astype(F32)
    bd = lambda x: _block_diag(x, bd_mask)

    pairs = range(n_pairs)
    sls = [slice(p * p2, (p + 1) * p2) for p in pairs]
    def segsum(xs):
        tot = _dot_sel2_r(jnp.concatenate(xs, axis=0), ones_bd)
        return [tot[i * CHUNK:(i + 1) * CHUNK] for i in range(len(xs))]

    units = [(j, p) for j in chunks for p in pairs]
    tile = lambda x, u: x[rows[u[0]], sls[u[1]]]
    r, kh, v, kkr = r_ref[...], kh_ref[...], v_ref[...], kkr_ref[...]
    rkk = r * kh * rk_ref[...]
    prep = {}

    def prep_norms():
        sums = segsum([tile(kkr * kkr, u) for u in units] + [tile(rkk, u) for u in units])
        ss = jnp.concatenate([jnp.concatenate(sums[j * n_pairs:(j + 1) * n_pairs], axis=1) for j in chunks], axis=0)
        kk = kkr / jnp.maximum(jnp.sqrt(ss), 1e-12)
        b = kk * a_ref[...]
        prep["bonus"] = sums[len(units):]
        prep["kk_t"], prep["r_t"] = kk * w_exc, r * w_inc
        prep["b_t"], prep["k_t"] = b * w_inv, kh * w_inv
        prep["k_e"], prep["b_e"] = kh * w_end, b * w_end

    def prep_gram():
        prep["bd_v"] = [bd(tile(v, u)).astype(BF16) for u in units]
        prep["lhs"] = [jnp.concatenate([tile(prep["kk_t"], u), tile(prep["r_t"], u)], axis=0).astype(BF16)
                       for u in units]
        prep["g_b"] = [_dot_nt(prep["lhs"][i], bd(tile(prep["b_t"], u))) for i, u in enumerate(units)]
        prep["g_k"] = [_dot_nt(prep["lhs"][i], bd(tile(prep["k_t"], u))) for i, u in enumerate(units)]

    def prep_products():
        g_b, g_k, bd_v = prep["g_b"], prep["g_k"], prep["bd_v"]
        idx = range(len(units))
        prep["rhs0"] = [_dot(jnp.where(strict, g_k[i][:CHUNK], 0.0), bd_v[i]) for i in idx]
        prep["okv"] = [_dot(jnp.where(incl, g_k[i][CHUNK:], 0.0), bd_v[i]) for i in idx]
        npow = [jnp.where(strict, -g_b[i][:CHUNK], 0.0) for i in idx]
        prep["tinv"] = [eye + npow[i] for i in idx]
        prep["npow"] = [_dot(npow[i], bd(npow[i])) for i in idx]

    def prep_double():
        npow, tinv = prep["npow"], prep["tinv"]
        both = [_dot(jnp.concatenate([npow[i], tinv[i]], axis=0), bd(npow[i])) for i in range(len(units))]
        prep["npow"] = [x[:CHUNK] for x in both]
        prep["tinv"] = [tinv[i] + both[i][CHUNK:] for i in range(len(units))]

    def prep_finish():
        npow, tinv = prep["npow"], prep["tinv"]
        prep["tinv"] = [tinv[i] + _dot(tinv[i], bd(npow[i])) for i in range(len(units))]

    state = [st_ref[p] for p in pairs]
    fin = {}

    def fin_from_state(j):
        c = RW_CB * (step - 1) + j
        seq, first, _ = _chunk_pos(jnp.maximum(c, 0), geom)
        first = jnp.logical_and(first, step >= 1)
        fin["seq"] = seq
        fin["st0"] = [jnp.where(first, si_ref[seq, p], state[p]) for p in pairs]
        fin["fs"] = [_dot_nt(lhs_s[rslot, j * n_pairs + p], fin["st0"][p]) for p in pairs]

    def fin_solve(j):
        fin["u"] = [_dot(tinv_s[rslot, j * n_pairs + p], bd(fin["fs"][p][:CHUNK] + rhs0_s[rslot, j * n_pairs + p]))
                    for p in pairs]

    def fin_out_state(j):
        fin["o"] = [fin["fs"][p][CHUNK:] + okv_s[rslot, j * n_pairs + p]
                    - _dot(brb_s[rslot, j * n_pairs + p], bd(fin["u"][p])) for p in pairs]
        for p in pairs:
            lhs_upd = jnp.concatenate([vb_s[rslot, rows[j], sls[p]], (-fin["u"][p]).astype(BF16)], axis=0)
            upd = _dot_tn(lhs_upd, keb_s[rslot, j * n_pairs + p])
            state[p] = fin["st0"][p] * decay_s[rslot, j, :, sls[p]] + jnp.where(bd_mask, upd, 0.0)
            so_ref[fin["seq"], p] = state[p]

    def fin_mean(j):
        mean = segsum(fin["o"])
        fin["oc"] = [fin["o"][p] - mean[p] * (1.0 / n) for p in pairs]

    def fin_write(j):
        var = segsum([x * x for x in fin["oc"]])
        for p in pairs:
            s = sls[p]
            on = fin["oc"][p] * lax.rsqrt(var[p] * (1.0 / n) + RW_GN_EPS) * lg_ref[:, s] + lb_ref[:, s]
            y_ref[rows[j], s] = (on + bonus_s[rslot, rows[j], s]) * g_ref[rows[j], s]

    prep_stages = [prep_norms, prep_gram, prep_products] + [prep_double] * 4 + [prep_finish]
    fin_stages = [functools.partial(f, j) for j in chunks
                  for f in (fin_from_state, fin_solve, fin_out_state, fin_mean, fin_write)]
    for k in range(max(len(prep_stages), len(fin_stages))):
        if k < len(fin_stages):
            fin_stages[k]()
        if k < len(prep_stages):
            prep_stages[k]()
    for p in pairs:
        st_ref[p] = state[p]

    for i, u in enumerate(units):
        lhs_s[wslot, i] = prep["lhs"][i]
        tinv_s[wslot, i] = prep["tinv"][i].astype(BF16)
        brb_s[wslot, i] = jnp.where(incl, prep["g_b"][i][CHUNK:], 0.0).astype(BF16)
        keb_s[wslot, i] = jnp.concatenate([tile(prep["k_e"], u), tile(prep["b_e"], u)], axis=0).astype(BF16)
        rhs0_s[wslot, i] = prep["rhs0"][i]
        okv_s[wslot, i] = prep["okv"][i]
        bonus_s[wslot, rows[u[0]], sls[u[1]]] = prep["bonus"][i] * tile(v, u)
    vb_s[wslot] = v.astype(BF16)
    for j in chunks:
        decay_s[wslot, j] = w_inc[rows[j]][CHUNK - 1:CHUNK]


def _rwkv(r, lw, kh, v, kkr, a, g, rk, lg, lb, s_in, geom: Geom):
    t, d = r.shape
    n = RW_N
    p2 = 2 * n
    n_pairs = d // p2
    n_seq = s_in.shape[0]
    assert geom.n_chunks % RW_CB == 0
    n_blocks = geom.n_chunks // RW_CB
    rb = RW_CB * CHUNK
    s4 = s_in.reshape(n_seq, n_pairs, 2, n, n)
    zero = jnp.zeros_like(s4[:, :, 0])
    s_bd = jnp.concatenate([jnp.concatenate([s4[:, :, 0], zero], axis=-1),
                            jnp.concatenate([zero, s4[:, :, 1]], axis=-1)], axis=-2)
    nxt = pl.BlockSpec((rb, d), lambda s: (jnp.minimum(s, n_blocks - 1), 0))
    cur = pl.BlockSpec((rb, d), lambda s: (jnp.maximum(s - 1, 0), 0))
    vec = _const_spec((1, d))
    per_unit = lambda rows, dtype: pltpu.VMEM((2, RW_CB * n_pairs, rows, p2), dtype)
    y, so = pl.pallas_call(
        functools.partial(_rwkv_kernel, geom=geom, n_pairs=n_pairs),
        grid=(n_blocks + 1,),
        in_specs=[nxt] * 6 + [cur] + [vec] * 3 + [_const_spec(s_bd.shape)],
        out_specs=[cur, _const_spec(s_bd.shape)],
        out_shape=[jax.ShapeDtypeStruct((t, d), F32), jax.ShapeDtypeStruct(s_bd.shape, F32)],
        scratch_shapes=[pltpu.VMEM((n_pairs, p2, p2), F32),
                        per_unit(p2, BF16), per_unit(CHUNK, BF16), per_unit(CHUNK, BF16), per_unit(p2, BF16),
                        pltpu.VMEM((2, rb, d), BF16), per_unit(CHUNK, F32), per_unit(CHUNK, F32),
                        pltpu.VMEM((2, rb, d), F32), pltpu.VMEM((2, RW_CB, 1, d), F32)],
        compiler_params=_params(),
        name="rwkv_scan",
    )(r, lw, kh, v, kkr, a, g, rk.reshape(1, d), lg.reshape(1, d), lb.reshape(1, d), s_bd)
    s_out = jnp.stack([so[:, :, :n, :n], so[:, :, n:, n:]], axis=2).reshape(s_in.shape)
    return y, s_out


def _rwkv_layer(x, gn, s_in, shift_in, p, j, geom):
    d = x[0].shape[1]
    seq_first = _first_chunks(geom)
    start_rows = jnp.zeros((geom.n_chunks, d), F32).at[seq_first].set(shift_in)
    r, lw, kh, v, kkr, a, g, hl = _rwkv_proj(x, gn, start_rows, p, j, geom)
    y, s_out = _rwkv(r, lw, kh, v, kkr, a, g, p["rw_rk"][j], p["rw_ln_g"][j], p["rw_ln_b"][j], s_in, geom)
    return y, s_out, hl.reshape(geom.n_chunks, d)[_last_chunks(geom)]


def _first_chunks(g: Geom):
    return np.concatenate([np.arange(g.nb_p) * g.cps_p,
                           g.npc + np.arange(g.n_seq - g.nb_p) * g.cps_s])


def _last_chunks(g: Geom):
    return _first_chunks(g) + np.concatenate([np.full(g.nb_p, g.cps_p - 1),
                                              np.full(g.n_seq - g.nb_p, g.cps_s - 1)])


def _gelu_tanh(x):
    return 0.5 * x * (1.0 + jnp.tanh(0.7978845608028654 * (x + 0.044715 * x * x * x)))


def _lru_proj_kernel(*refs, n_x, np_tiles):
    x_refs, (gn_ref, wy_ref, wx_ref, y_ref, ux_ref) = refs[:n_x], refs[n_x:]
    h = _rms(_slab_tile(x_refs, np_tiles), gn_ref[...]).astype(BF16)
    y_ref[...] = _gelu_tanh(jnp.dot(h, wy_ref[...], preferred_element_type=F32))
    ux_ref[...] = jnp.dot(h, wx_ref[...], preferred_element_type=F32)


def _lru_proj(x, gn, wy, wx, *, tm=512):
    d = x[0].shape[1]
    rows, np_tiles, n_tiles = _slab_specs([a.shape for a in x], tm)
    out = jax.ShapeDtypeStruct((n_tiles * tm, wy.shape[1]), F32)
    return pl.pallas_call(
        functools.partial(_lru_proj_kernel, n_x=len(x), np_tiles=np_tiles),
        grid=(n_tiles,),
        in_specs=rows + [_const_spec((1, d)), _const_spec(wy.shape), _const_spec(wx.shape)],
        out_specs=[pl.BlockSpec((tm, wy.shape[1]), lambda i: (i, 0))] * 2,
        out_shape=[out, out],
        compiler_params=_params("parallel"),
        name="lru_proj",
    )(*x, gn.reshape(1, d), wy.astype(BF16), wx.astype(BF16))


def _shift_rows(x, d, fill, row):
    return jnp.where(row < d, fill, pltpu.roll(x, d, 0))


def _lru_kernel(ux_ref, y_ref, cw_ref, cb_ref, gaw_ref, gab_ref, gxw_ref, gxb_ref, lam_ref,
                hi_ref, bi_ref, hy_ref, ho_ref, bo_ref, win_ref, hc_ref, *, geom, conv_k):
    c = pl.program_id(0)
    _, first, last = _chunk_pos(c, geom)
    top = SUBLANES
    tail = conv_k - 1

    @pl.when(first)
    def _():
        win_ref[top - tail:top] = bi_ref[0]
        hc_ref[...] = hi_ref[0]

    win_ref[top:] = ux_ref[...]
    _window_conv(win_ref, cw_ref, cb_ref, hy_ref, top - tail, conv_k)
    u = hy_ref[...]

    @pl.when(last)
    def _():
        bo_ref[0] = win_ref[top + CHUNK - tail:top + CHUNK]

    win_ref[:top] = win_ref[CHUNK:CHUNK + top]

    bw = u.shape[1] // LRU_BLOCKS
    zr, zi = [], []
    for blk in range(LRU_BLOCKS):
        ub = u[:, blk * bw:(blk + 1) * bw].astype(BF16)
        zr.append(jnp.dot(ub, gaw_ref[blk], preferred_element_type=F32))
        zi.append(jnp.dot(ub, gxw_ref[blk], preferred_element_type=F32))
    r = _sigmoid(jnp.concatenate(zr, axis=1) + gab_ref[...])
    i = _sigmoid(jnp.concatenate(zi, axis=1) + gxb_ref[...])
    nl = -lam_ref[...]
    softplus = jnp.maximum(nl, 0.0) + jnp.log1p(jnp.exp(-jnp.abs(nl)))
    log_a = -LRU_C * r * softplus
    a = jnp.exp(log_a)
    bterm = jnp.sqrt(-jnp.tanh(log_a) * (a * a + 1.0)) * (i * u)

    row = lax.broadcasted_iota(jnp.int32, a.shape, 0)
    d = 1
    while d < CHUNK:
        a_s = _shift_rows(a, d, 1.0, row)
        b_s = _shift_rows(bterm, d, 0.0, row)
        bterm = bterm + a * b_s
        a = a * a_s
        d *= 2
    h = a * hc_ref[...] + bterm
    hy_ref[...] = h * y_ref[...]
    hc_ref[...] = h[CHUNK - 1:CHUNK]

    @pl.when(last)
    def _():
        ho_ref[0] = h[CHUNK - 1:CHUNK]


def _lru(ux, y, h_in, buf_in, p, j, geom: Geom):
    t, w = ux.shape
    conv_k = p["lru_conv_w"].shape[1]
    row = pl.BlockSpec((CHUNK, w), lambda c: (c, 0))
    vec = _const_spec((1, w))
    gspec = _const_spec(p["lru_ga_w"].shape[1:])
    hspec = pl.BlockSpec((1, 1, w), _seq_index_map(geom, 3))
    bspec = pl.BlockSpec((1, conv_k - 1, w), _seq_index_map(geom, 3))
    h3 = h_in.reshape(geom.n_seq, 1, w)
    return pl.pallas_call(
        functools.partial(_lru_kernel, geom=geom, conv_k=conv_k),
        grid=(geom.n_chunks,),
        in_specs=[row, row, _const_spec((conv_k, w)), vec, gspec, vec, gspec, vec, vec, hspec, bspec],
        out_specs=[row, hspec, bspec],
        out_shape=[jax.ShapeDtypeStruct((t, w), F32), jax.ShapeDtypeStruct(h3.shape, F32),
                   jax.ShapeDtypeStruct(buf_in.shape, F32)],
        scratch_shapes=[pltpu.VMEM((CHUNK + SUBLANES, w), F32), pltpu.VMEM((1, w), F32)],
        compiler_params=_params(),
        name="lru_scan",
    )(ux, y, p["lru_conv_w"][j], p["lru_conv_b"][j].reshape(1, w), p["lru_ga_w"][j].astype(BF16),
      p["lru_ga_b"][j].reshape(1, w), p["lru_gx_w"][j].astype(BF16), p["lru_gx_b"][j].reshape(1, w),
      p["lru_lam"][j].reshape(1, w), h3, buf_in)


def _lru_layer(x, gn, h_in, buf_in, p, j, geom):
    y, ux = _lru_proj(x, gn, p["lru_wy"][j], p["lru_wx"][j])
    hy, h_out, buf_out = _lru(ux, y, h_in, buf_in, p, j, geom)
    return hy, h_out.reshape(h_in.shape), buf_out


def _conf_proj_kernel(*refs, n_x, np_tiles):
    x_refs, (gn_ref, w1_ref, b1_ref, u_ref) = refs[:n_x], refs[n_x:]
    d = u_ref.shape[1]
    h = _rms(_slab_tile(x_refs, np_tiles), gn_ref[...]).astype(BF16)
    hh = jnp.dot(h, w1_ref[...], preferred_element_type=F32) + b1_ref[...]
    u_ref[...] = hh[:, :d] * _sigmoid(hh[:, d:])


def _conf_proj(x, gn, w1, b1, *, tm=512):
    d = x[0].shape[1]
    rows, np_tiles, n_tiles = _slab_specs([a.shape for a in x], tm)
    return pl.pallas_call(
        functools.partial(_conf_proj_kernel, n_x=len(x), np_tiles=np_tiles),
        grid=(n_tiles,),
        in_specs=rows + [_const_spec((1, d)), _const_spec(w1.shape), _const_spec((1, w1.shape[1]))],
        out_specs=pl.BlockSpec((tm, d), lambda i: (i, 0)),
        out_shape=jax.ShapeDtypeStruct((n_tiles * tm, d), F32),
        compiler_params=_params("parallel"),
        name="conf_proj",
    )(*x, gn.reshape(1, d), w1.astype(BF16), b1.reshape(1, -1))


def _window_conv(win_ref, w_ref, b_ref, out_ref, base, conv_k):
    for lt in range(out_ref.shape[1] // LANES):
        sl = slice(lt * LANES, (lt + 1) * LANES)
        acc = jnp.broadcast_to(b_ref[:, sl], (CHUNK, LANES))
        for b in range(SUBLANES):
            offs = [o for o in range(base, base + conv_k) if o % SUBLANES == b]
            if not offs:
                continue
            rows = CHUNK + SUBLANES if b else CHUNK
            part = sum(w_ref[o - base:o - base + 1, sl] * win_ref[o - b:o - b + rows, sl] for o in offs)
            acc = acc + (pltpu.roll(part, rows - b, 0)[:CHUNK] if b else part)
        out_ref[:, sl] = acc


def _conf_kernel(u_ref, w_ref, b_ref, lg_ref, lb_ref, bi_ref, c_ref, bo_ref, win_ref, *, geom, conv_k, top):
    c = pl.program_id(0)
    _, first, last = _chunk_pos(c, geom)
    tail = conv_k - 1

    @pl.when(first)
    def _():
        win_ref[top - tail:top] = bi_ref[0]

    win_ref[top:] = u_ref[...]
    _window_conv(win_ref, w_ref, b_ref, c_ref, top - tail, conv_k)
    acc = c_ref[...]

    @pl.when(last)
    def _():
        bo_ref[0] = win_ref[top + CHUNK - tail:top + CHUNK]

    win_ref[:top] = win_ref[CHUNK:CHUNK + top]
    mu = jnp.mean(acc, axis=-1, keepdims=True)
    xc = acc - mu
    var = jnp.mean(xc * xc, axis=-1, keepdims=True)
    c_ref[...] = _silu(xc * lax.rsqrt(var + EPS) * lg_ref[...] + lb_ref[...])


def _conf(u, buf_in, p, j, geom: Geom):
    t, d = u.shape
    conv_k = p["cf_dw_w"].shape[1]
    top = -(-(conv_k - 1) // SUBLANES) * SUBLANES
    assert top <= CHUNK
    row = pl.BlockSpec((CHUNK, d), lambda c: (c, 0))
    vec = _const_spec((1, d))
    bspec = pl.BlockSpec((1, conv_k - 1, d), _seq_index_map(geom, 3))
    return pl.pallas_call(
        functools.partial(_conf_kernel, geom=geom, conv_k=conv_k, top=top),
        grid=(geom.n_chunks,),
        in_specs=[row, _const_spec((conv_k, d)), vec, vec, vec, bspec],
        out_specs=[row, bspec],
        out_shape=[jax.ShapeDtypeStruct((t, d), F32), jax.ShapeDtypeStruct(buf_in.shape, F32)],
        scratch_shapes=[pltpu.VMEM((CHUNK + top, d), F32)],
        compiler_params=_params(),
        name="conf_conv",
    )(u, p["cf_dw_w"][j], p["cf_dw_b"][j].reshape(1, d), p["cf_ln_g"][j].reshape(1, d),
      p["cf_ln_b"][j].reshape(1, d), buf_in)


def _conf_layer(x, gn, buf_in, p, j, geom):
    u = _conf_proj(x, gn, p["cf_w1"][j], p["cf_b1"][j])
    return _conf(u, buf_in, p, j, geom)


_ARG_NAMES = (
    "x_prompt x_sample state_hgrn state_rwkv state_rwkv_shift state_lru state_lru_conv state_conf_conv "
    "norm_mix norm_ffn norm_final hg_wq hg_wf hg_wi hg_wg hg_gn hg_wo hg_lb rw_mu rw_wr rw_wk rw_wv rw_w0 "
    "rw_w1 rw_w2 rw_a0 rw_a1 rw_a2 rw_g1 rw_g2 rw_kk rw_ka rw_rk rw_ln_g rw_ln_b rw_wo lru_wy lru_wx "
    "lru_conv_w lru_conv_b lru_ga_w lru_ga_b lru_gx_w lru_gx_b lru_lam lru_wo cf_w1 cf_b1 cf_dw_w cf_dw_b "
    "cf_ln_g cf_ln_b cf_w2 cf_b2 ffn_w1 ffn_w3 ffn_w2").split()
N_MIXERS = 4


def kernel(x_prompt, x_sample, state_hgrn, state_rwkv, state_rwkv_shift, state_lru, state_lru_conv, state_conf_conv, norm_mix, norm_ffn, norm_final, hg_wq, hg_wf, hg_wi, hg_wg, hg_gn, hg_wo, hg_lb, rw_mu, rw_wr, rw_wk, rw_wv, rw_w0, rw_w1, rw_w2, rw_a0, rw_a1, rw_a2, rw_g1, rw_g2, rw_kk, rw_ka, rw_rk, rw_ln_g, rw_ln_b, rw_wo, lru_wy, lru_wx, lru_conv_w, lru_conv_b, lru_ga_w, lru_ga_b, lru_gx_w, lru_gx_b, lru_lam, lru_wo, cf_w1, cf_b1, cf_dw_w, cf_dw_b, cf_ln_g, cf_ln_b, cf_w2, cf_b2, ffn_w1, ffn_w3, ffn_w2):
    p = dict(zip(_ARG_NAMES, (x_prompt, x_sample, state_hgrn, state_rwkv, state_rwkv_shift, state_lru, state_lru_conv, state_conf_conv, norm_mix, norm_ffn, norm_final, hg_wq, hg_wf, hg_wi, hg_wg, hg_gn, hg_wo, hg_lb, rw_mu, rw_wr, rw_wk, rw_wv, rw_w0, rw_w1, rw_w2, rw_a0, rw_a1, rw_a2, rw_g1, rw_g2, rw_kk, rw_ka, rw_rk, rw_ln_g, rw_ln_b, rw_wo, lru_wy, lru_wx, lru_conv_w, lru_conv_b, lru_ga_w, lru_ga_b, lru_gx_w, lru_gx_b, lru_lam, lru_wo, cf_w1, cf_b1, cf_dw_w, cf_dw_b, cf_ln_g, cf_ln_b, cf_w2, cf_b2, ffn_w1, ffn_w3, ffn_w2)))
    nb_p, seq_p, d = x_prompt.shape
    nb_s, seq_s, _ = x_sample.shape
    assert seq_p % CHUNK == 0 and seq_s % CHUNK == 0
    npc = nb_p * seq_p // CHUNK
    geom = Geom(n_chunks=npc + nb_s * seq_s // CHUNK, npc=npc, cps_p=seq_p // CHUNK,
                cps_s=seq_s // CHUNK, nb_p=nb_p, n_seq=nb_p + nb_s)
    x = (x_prompt.reshape(-1, d), x_sample.reshape(-1, d))
    split_shapes = [a.shape for a in x]
    n_rows = sum(s[0] for s in split_shapes)

    def all_seqs(state):
        return jnp.concatenate([jnp.zeros((nb_p,) + state.shape[1:], state.dtype), state], axis=0)

    depth = norm_mix.shape[0]
    zero_bias = jnp.zeros((d,), F32)
    outs = {k: [] for k in ("hg", "rw", "sh", "lh", "lc", "cf")}
    for i in range(depth):
        m, j = i % N_MIXERS, i // N_MIXERS
        gn = norm_mix[i]
        bo = zero_bias
        if m == 0:
            y, s = _hgrn_layer(x, gn, all_seqs(state_hgrn[j]), p, j, geom)
            outs["hg"].append(s)
            wo = hg_wo[j]
        elif m == 1:
            y, s, sh = _rwkv_layer(x, gn, all_seqs(state_rwkv[j]), all_seqs(state_rwkv_shift[j]), p, j, geom)
            outs["rw"].append(s)
            outs["sh"].append(sh)
            wo = rw_wo[j]
        elif m == 2:
            y, hl, cb = _lru_layer(x, gn, all_seqs(state_lru[j]), all_seqs(state_lru_conv[j]), p, j, geom)
            outs["lh"].append(hl)
            outs["lc"].append(cb)
            wo = lru_wo[j]
        else:
            y, cb = _conf_layer(x, gn, all_seqs(state_conf_conv[j]), p, j, geom)
            outs["cf"].append(cb)
            wo, bo = cf_w2[j], cf_b2[j]
        final = i == depth - 1
        out_shapes = split_shapes if final else [(n_rows, d)]
        x = _post_ffn(x, y, wo, bo, norm_ffn[i], ffn_w1[i], ffn_w3[i], ffn_w2[i], norm_final,
                      final=final, out_shapes=out_shapes)

    stacked = [jnp.stack(outs[k]) for k in ("hg", "rw", "sh", "lh", "lc", "cf")]
    return ((x[0].reshape(nb_p, seq_p, d), x[1].reshape(nb_s, seq_s, d))
            + tuple(s[:, :nb_p] for s in stacked) + tuple(s[:, nb_p:] for s in stacked))
```

```python
import functools
from typing import NamedTuple

import numpy as np
import jax
import jax.numpy as jnp
from jax import lax
from jax.experimental import pallas as pl
from jax.experimental.pallas import tpu as pltpu

F32 = jnp.float32
BF16 = jnp.bfloat16

EPS = 1e-6
CHUNK = 64
SUB = 16
LANES = 128
SUBLANES = 8
HG_DK = 128
RW_N = 64
RW_CB = 2
HG_CB = 2
CONV_CB = 4
RW_DECAY_SCALE = 0.6065306597126334
RW_GN_EPS = 64e-5
LRU_BLOCKS = 8
LRU_C = 8.0
VMEM_LIMIT = 56 * 1024 * 1024


class Geom(NamedTuple):
    n_chunks: int
    npc: int
    cps_p: int
    cps_s: int
    nb_p: int
    n_seq: int


def _chunk_pos(c, g: Geom):
    in_p = c < g.npc
    cs = c - g.npc
    seq = jnp.where(in_p, c // g.cps_p, g.nb_p + cs // g.cps_s)
    first = jnp.where(in_p, c % g.cps_p == 0, cs % g.cps_s == 0)
    last = jnp.where(in_p, c % g.cps_p == g.cps_p - 1, cs % g.cps_s == g.cps_s - 1)
    return seq, first, last


def _dot(a, b):
    return jnp.dot(a.astype(BF16), b.astype(BF16), preferred_element_type=F32)


def _dot_nt(a, b):
    return lax.dot_general(a.astype(BF16), b.astype(BF16), (((1,), (1,)), ((), ())),
                           preferred_element_type=F32)


def _dot_tn(a, b):
    return lax.dot_general(a.astype(BF16), b.astype(BF16), (((0,), (0,)), ((), ())),
                           preferred_element_type=F32)


def _split3(x):
    hi = x.astype(BF16)
    r1 = x - hi.astype(F32)
    mid = r1.astype(BF16)
    lo = (r1 - mid.astype(F32)).astype(BF16)
    return hi, mid, lo


def _dot_sel(sel_bf16, x):
    hi, mid, lo = _split3(x)
    d = lambda t: jnp.dot(sel_bf16, t, preferred_element_type=F32)
    return d(hi) + d(mid) + d(lo)


def _dot_sel2_r(x, sel_bf16):
    hi = x.astype(BF16)
    lo = (x - hi.astype(F32)).astype(BF16)
    return (jnp.dot(hi, sel_bf16, preferred_element_type=F32)
            + jnp.dot(lo, sel_bf16, preferred_element_type=F32))


def _rms(x, g):
    return x * lax.rsqrt(jnp.mean(x * x, axis=-1, keepdims=True) + EPS) * g


def _sigmoid(x):
    return 0.5 * jnp.tanh(0.5 * x) + 0.5


def _silu(x, scale=1.0):
    return (0.5 * scale) * x * (jnp.tanh(0.5 * x) + 1.0)


def _tri_incl(n):
    r = lax.broadcasted_iota(jnp.int32, (n, n), 0)
    c = lax.broadcasted_iota(jnp.int32, (n, n), 1)
    return (r >= c).astype(BF16)


def _const_spec(shape):
    return pl.BlockSpec(shape, lambda *_: (0,) * len(shape), pipeline_mode=pl.Buffered(1))


def _slab_specs(shapes, tm):
    d = shapes[0][1]
    assert all(s[0] % tm == 0 for s in shapes)
    if len(shapes) == 1:
        n_tiles = shapes[0][0] // tm
        return [pl.BlockSpec((tm, d), lambda i: (i, 0))], n_tiles, n_tiles
    np_tiles = shapes[0][0] // tm
    specs = [pl.BlockSpec((tm, d), lambda i: (jnp.minimum(i, np_tiles - 1), 0)),
             pl.BlockSpec((tm, d), lambda i: (jnp.maximum(i - np_tiles, 0), 0))]
    return specs, np_tiles, np_tiles + shapes[1][0] // tm


def _slab_tile(x_refs, np_tiles):
    if len(x_refs) == 1:
        return x_refs[0][...]
    return jnp.where(pl.program_id(0) < np_tiles, x_refs[0][...], x_refs[1][...])


def _params(sem="arbitrary"):
    return pltpu.CompilerParams(dimension_semantics=(sem,), vmem_limit_bytes=VMEM_LIMIT)


def _post_ffn_kernel(*refs, n_x, final, np_in, np_out):
    x_refs, (y_ref, wo_ref, bo_ref, gn_ref, w1_ref, w3_ref, w2_ref, gf_ref) = refs[:n_x], refs[n_x:n_x + 8]
    o_refs = refs[n_x + 8:]
    x1 = _slab_tile(x_refs, np_in) + _dot(y_ref[...], wo_ref[...]) + bo_ref[...]
    h = _rms(x1, gn_ref[...]).astype(BF16)
    a = jnp.dot(h, w1_ref[...], preferred_element_type=F32)
    b = jnp.dot(h, w3_ref[...], preferred_element_type=F32)
    out = x1 + _dot(_silu(a) * b, w2_ref[...])
    if final:
        out = _rms(out, gf_ref[...])
    if len(o_refs) == 1:
        o_refs[0][...] = out
    else:
        in_first = pl.program_id(0) < np_out

        @pl.when(in_first)
        def _():
            o_refs[0][...] = out

        @pl.when(jnp.logical_not(in_first))
        def _():
            o_refs[1][...] = out


def _post_ffn(x, y, wo, bo, gn, w1, w3, w2, gf, *, final, out_shapes, tm=512):
    d = x[0].shape[1]
    rows, np_in, n_tiles = _slab_specs([a.shape for a in x], tm)
    out_rows, np_out, _ = _slab_specs(out_shapes, tm)
    return pl.pallas_call(
        functools.partial(_post_ffn_kernel, n_x=len(x), final=final, np_in=np_in, np_out=np_out),
        grid=(n_tiles,),
        in_specs=rows + [pl.BlockSpec((tm, y.shape[1]), lambda i: (i, 0)),
                         _const_spec(wo.shape), _const_spec((1, d)), _const_spec((1, d)),
                         _const_spec(w1.shape), _const_spec(w3.shape), _const_spec(w2.shape),
                         _const_spec((1, d))],
        out_specs=out_rows,
        out_shape=[jax.ShapeDtypeStruct(s, F32) for s in out_shapes],
        compiler_params=_params(),
        name="post_ffn",
    )(*x, y, wo.astype(BF16), bo.reshape(1, d), gn.reshape(1, d), w1.astype(BF16), w3.astype(BF16),
      w2.astype(BF16), gf.reshape(1, d))


def _hgrn_proj_kernel(*refs, n_x, layer, np_tiles):
    x_refs = refs[:n_x]
    (gn_ref, lbp_ref, wq_ref, wf_ref, wi_ref, wg_ref, q_ref, lk_ref, lf_ref, v_ref, gate_ref) = refs[n_x:]
    h = _rms(_slab_tile(x_refs, np_tiles), gn_ref[...]).astype(BF16)
    lbp = lbp_ref[...]
    e = jnp.exp(lbp - jnp.max(lbp, axis=0, keepdims=True))
    lb = jnp.sum(e[:layer + 1], axis=0, keepdims=True) / jnp.sum(e, axis=0, keepdims=True)
    q_ref[...] = _silu(jnp.dot(h, wq_ref[...], preferred_element_type=F32), HG_DK ** -0.5)
    fl = jnp.dot(h, wf_ref[...], preferred_element_type=F32)
    l1p = jnp.log1p(jnp.exp(-jnp.abs(fl)))
    lk_ref[...] = jnp.log(1.0 - lb) - (jnp.maximum(fl, 0.0) + l1p)
    sig = jnp.exp(jnp.minimum(fl, 0.0) - l1p)
    lf_ref[...] = jnp.log(lb + (1.0 - lb) * sig)
    v_ref[...] = jnp.dot(h, wi_ref[...], preferred_element_type=F32)
    gate_ref[...] = _silu(jnp.dot(h, wg_ref[...], preferred_element_type=F32))


def _hgrn_proj(x, gn, lbp, wq, wf, wi, wg, *, layer, tm=512):
    d = x[0].shape[1]
    rows, np_tiles, n_tiles = _slab_specs([a.shape for a in x], tm)
    row = pl.BlockSpec((tm, d), lambda i: (i, 0))
    wspec = _const_spec((d, d))
    out = jax.ShapeDtypeStruct((n_tiles * tm, d), F32)
    return pl.pallas_call(
        functools.partial(_hgrn_proj_kernel, n_x=len(x), layer=layer, np_tiles=np_tiles),
        grid=(n_tiles,),
        in_specs=rows + [_const_spec((1, d)), _const_spec(lbp.shape), wspec, wspec, wspec, wspec],
        out_specs=[row] * 5,
        out_shape=[out] * 5,
        compiler_params=_params("parallel"),
        name="hgrn_proj",
    )(*x, gn.reshape(1, d), lbp, wq.astype(BF16), wf.astype(BF16), wi.astype(BF16), wg.astype(BF16))


def _gla_kernel(q_ref, lk_ref, lf_ref, v_ref, gate_ref, gn_ref, si_ref, og_ref, so_ref,
                st_ref, zp_ref, *, geom, n_heads):
    step = pl.program_id(0)
    chunks = range(HG_CB)
    rows = [slice(j * CHUNK, (j + 1) * CHUNK) for j in chunks]

    @pl.when(step == 0)
    def _():
        st_ref[...] = jnp.zeros(st_ref.shape, F32)

    tri = _tri_incl(CHUNK)
    lf_all = lf_ref[...]
    cum = jnp.concatenate([_dot_sel(tri, lf_all[rj]) for rj in rows], axis=0)
    z_all = lk_ref[...] - cum
    for j in chunks:
        zp_ref[j, :SUB] = jnp.zeros((SUB, zp_ref.shape[2]), F32)
        zp_ref[j, SUB:] = z_all[rows[j]]
    ones = jnp.ones((HG_DK, LANES), BF16)
    t_idx = lax.broadcasted_iota(jnp.int32, (CHUNK, LANES), 0)
    s_idx = lax.broadcasted_iota(jnp.int32, (CHUNK, LANES), 1)
    pair_delta = jnp.where((s_idx <= t_idx) & (s_idx // SUB == t_idx // SUB), t_idx - s_idx, -1)
    n_sub = CHUNK // SUB
    batch = 4

    heads = range(n_heads)
    sls = [slice(h * HG_DK, (h + 1) * HG_DK) for h in heads]
    units = [(j, h) for j in chunks for h in heads]
    tile = lambda x, u: x[rows[u[0]], sls[u[1]]]
    q_all, v_all = q_ref[...], v_ref[...]
    last_cum = jnp.concatenate([jnp.broadcast_to(cum[rj][CHUNK - 1:CHUNK], (CHUNK, cum.shape[1])) for rj in rows],
                               axis=0)
    q_in = q_all * jnp.exp(cum)
    k_out = jnp.exp(z_all + last_cum)
    upd = [_dot_tn(tile(v_all, u), tile(k_out, u)) for u in units]
    blocks = [[jnp.zeros((SUB, HG_DK), F32)] for _ in units]
    for i in range(1, n_sub):
        lo, hi = i * SUB, (i + 1) * SUB
        sc = []
        for j, h in units:
            cj, zj, qj = cum[rows[j]][:, sls[h]], z_all[rows[j]][:, sls[h]], q_all[rows[j]][:, sls[h]]
            ref = cj[lo - 1:lo]
            sc.append(_dot_nt(qj[lo:hi] * jnp.exp(cj[lo:hi] - ref), jnp.exp(zj[:lo] + ref)))
        for i_u, u in enumerate(units):
            blocks[i_u].append(_dot(sc[i_u], tile(v_all, u)[:lo]))
    scores = [jnp.zeros((CHUNK, LANES), F32) for _ in units]
    for b0 in range(0, SUBLANES, batch):
        deltas = [d for b in range(b0, b0 + batch) for d in (b, b + SUBLANES)]
        sums = []
        for j, h in units:
            parts = []
            for b in range(b0, b0 + batch):
                zb = zp_ref[j, SUBLANES - b:SUBLANES - b + CHUNK + SUBLANES, sls[h]]
                for zs in (zb[SUBLANES:], zb[:CHUNK]):
                    parts.append((tile(q_all, (j, h)) * jnp.exp(tile(cum, (j, h)) + zs)).astype(BF16))
            sums.append(jnp.dot(jnp.concatenate(parts, axis=0), ones, preferred_element_type=F32))
        for i_u in range(len(units)):
            for i, delta in enumerate(deltas):
                scores[i_u] = jnp.where(pair_delta == delta, sums[i_u][i * CHUNK:(i + 1) * CHUNK], scores[i_u])
    o_local = [jnp.concatenate(blocks[i_u], axis=0) + _dot(scores[i_u][:, :CHUNK], tile(v_all, u))
               for i_u, u in enumerate(units)]

    state = [st_ref[h] for h in heads]
    decay = jnp.exp(last_cum)
    for j in chunks:
        seq, first, _ = _chunk_pos(step * HG_CB + j, geom)
        for h in heads:
            st0 = jnp.where(first, si_ref[seq, h], state[h])
            o = o_local[j * n_heads + h] + _dot_nt(tile(q_in, (j, h)), st0)
            state[h] = st0 * tile(decay, (j, h))[:1] + upd[j * n_heads + h]
            so_ref[seq, h] = state[h]
            on = o * lax.rsqrt(jnp.mean(o * o, axis=-1, keepdims=True) + EPS) * gn_ref[...]
            og_ref[rows[j], sls[h]] = on * gate_ref[rows[j], sls[h]]
    for h in heads:
        st_ref[h] = state[h]


def _gla(q, lk, lf, v, gate, gn, s_in, geom: Geom):
    t, d = q.shape
    n_heads = d // HG_DK
    assert geom.n_chunks % HG_CB == 0
    rb = HG_CB * CHUNK
    row = pl.BlockSpec((rb, d), lambda c: (c, 0))
    s_t = jnp.swapaxes(s_in, -1, -2)
    og, so = pl.pallas_call(
        functools.partial(_gla_kernel, geom=geom, n_heads=n_heads),
        grid=(geom.n_chunks // HG_CB,),
        in_specs=[row] * 5 + [_const_spec((1, HG_DK)), _const_spec(s_t.shape)],
        out_specs=[row, _const_spec(s_t.shape)],
        out_shape=[jax.ShapeDtypeStruct((t, d), F32), jax.ShapeDtypeStruct(s_t.shape, F32)],
        scratch_shapes=[pltpu.VMEM((n_heads, HG_DK, HG_DK), F32), pltpu.VMEM((HG_CB, CHUNK + SUB, d), F32)],
        compiler_params=_params(),
        name="hgrn_gla",
    )(q, lk, lf, v, gate, gn.reshape(1, HG_DK), s_t)
    return og, jnp.swapaxes(so, -1, -2)


def _hgrn_layer(x, gn, s_in, p, j, geom):
    q, lk, lf, v, gate = _hgrn_proj(x, gn, p["hg_lb"], p["hg_wq"][j], p["hg_wf"][j],
                                    p["hg_wi"][j], p["hg_wg"][j], layer=j)
    return _gla(q, lk, lf, v, gate, p["hg_gn"][j], s_in, geom)


def _rwkv_proj_kernel(*refs, n_x, geom, tm, np_tiles):
    x_refs, prev_refs = refs[:n_x], refs[n_x:2 * n_x]
    (st_ref, gn_ref, mu_ref, wr_ref, wk_ref, wv_ref, w0_ref, w1_ref, w2_ref, a0_ref, a1_ref, a2_ref,
     g1_ref, g2_ref, kk_ref, ka_ref,
     r_ref, lw_ref, kh_ref, v_ref, kkr_ref, a_ref, g_ref, hl_ref, win_ref) = refs[2 * n_x:]
    i = pl.program_id(0)
    cpt = tm // CHUNK
    gn = gn_ref[...]
    h = _rms(_slab_tile(x_refs, np_tiles), gn)
    win_ref[SUBLANES:] = h
    win_ref[:SUBLANES] = _rms(_slab_tile(prev_refs, np_tiles), gn)
    for j in range(cpt):
        last_row = SUBLANES + (j + 1) * CHUNK - 1
        hl_ref[j] = win_ref[last_row:last_row + 1]
    for j in range(cpt):
        _, first, _ = _chunk_pos(i * cpt + j, geom)

        @pl.when(first)
        def _():
            row = SUBLANES - 1 + j * CHUNK
            win_ref[row:row + 1] = st_ref[j]

    xx = win_ref[SUBLANES - 1:SUBLANES - 1 + tm] - h
    mix = lambda n: (h + xx * mu_ref[n:n + 1]).astype(BF16)
    dot = lambda a, w_ref: jnp.dot(a, w_ref[...], preferred_element_type=F32)
    r_ref[...] = dot(mix(0), wr_ref)
    k = dot(mix(2), wk_ref)
    v_ref[...] = dot(mix(3), wv_ref)
    tw = jnp.tanh(dot(mix(1), w1_ref)).astype(BF16)
    lw_ref[...] = -RW_DECAY_SCALE * _sigmoid(w0_ref[...] + dot(tw, w2_ref))
    a = _sigmoid(a0_ref[...] + dot(dot(mix(4), a1_ref).astype(BF16), a2_ref))
    g_ref[...] = dot(_sigmoid(dot(mix(5), g1_ref)).astype(BF16), g2_ref)
    a_ref[...] = a
    kkr_ref[...] = k * kk_ref[...]
    kh_ref[...] = k * (1.0 + (a - 1.0) * ka_ref[...])


def _rwkv_proj(x, gn, start_rows, p, j, geom, *, tm=256):
    d = x[0].shape[1]
    rows, np_tiles, n_tiles = _slab_specs([a.shape for a in x], tm)
    t = n_tiles * tm
    cpt = tm // CHUNK
    row = pl.BlockSpec((tm, d), lambda i: (i, 0))
    bpt = tm // SUBLANES
    prev = [pl.BlockSpec((SUBLANES, d), lambda i: (jnp.clip(i * bpt - 1, 0, np_tiles * bpt - 1), 0))]
    if len(x) == 2:
        prev.append(pl.BlockSpec((SUBLANES, d), lambda i: (jnp.maximum((i - np_tiles) * bpt - 1, 0), 0)))
    vec = _const_spec((1, d))
    bf = lambda w: w.astype(BF16)
    ws = [bf(p["rw_wr"][j]), bf(p["rw_wk"][j]), bf(p["rw_wv"][j]), p["rw_w0"][j].reshape(1, d),
          bf(p["rw_w1"][j]), bf(p["rw_w2"][j]), p["rw_a0"][j].reshape(1, d), bf(p["rw_a1"][j]),
          bf(p["rw_a2"][j]), bf(p["rw_g1"][j]), bf(p["rw_g2"][j]), p["rw_kk"][j].reshape(1, d),
          p["rw_ka"][j].reshape(1, d)]
    out = jax.ShapeDtypeStruct((t, d), F32)
    return pl.pallas_call(
        functools.partial(_rwkv_proj_kernel, n_x=len(x), geom=geom, tm=tm, np_tiles=np_tiles),
        grid=(n_tiles,),
        in_specs=rows + prev + [pl.BlockSpec((cpt, 1, d), lambda i: (i, 0, 0)), vec,
                                _const_spec((6, d))] + [_const_spec(w.shape) for w in ws],
        out_specs=[row] * 7 + [pl.BlockSpec((cpt, 1, d), lambda i: (i, 0, 0))],
        out_shape=[out] * 7 + [jax.ShapeDtypeStruct((geom.n_chunks, 1, d), F32)],
        scratch_shapes=[pltpu.VMEM((tm + SUBLANES, d), F32)],
        compiler_params=_params(),
        name="rwkv_proj",
    )(*x, *x, start_rows.reshape(geom.n_chunks, 1, d), gn.reshape(1, d), p["rw_mu"][j], *ws)


def _block_diag(x, bd_mask):
    return jnp.where(bd_mask, jnp.concatenate([x, x], axis=0), 0.0)


def _rwkv_kernel(r_ref, lw_ref, kh_ref, v_ref, kkr_ref, a_ref, g_ref, rk_ref, lg_ref, lb_ref, si_ref,
                 y_ref, so_ref, st_ref, lhs_s, tinv_s, brb_s, keb_s, vb_s, rhs0_s, okv_s, bonus_s, decay_s,
                 *, geom, n_pairs):
    step = pl.program_id(0)
    wslot = step % 2
    rslot = 1 - wslot
    n = RW_N
    p2 = 2 * n

    @pl.when(step == 0)
    def _():
        for ref in (lhs_s, tinv_s, brb_s, keb_s, vb_s, rhs0_s, okv_s, bonus_s, decay_s):
            ref[1] = jnp.zeros(ref.shape[1:], ref.dtype)
        st_ref[...] = jnp.zeros(st_ref.shape, F32)

    chunks = range(RW_CB)
    rows = [slice(j * CHUNK, (j + 1) * CHUNK) for j in chunks]
    lw = lw_ref[...]
    tri = _tri_incl(CHUNK)
    cw = jnp.concatenate([_dot_sel(tri, lw[rj]) for rj in rows], axis=0)
    w_inc = jnp.exp(cw)
    w_exc = jnp.exp(cw - lw)
    w_inv = jnp.exp(-cw)
    w_end = jnp.exp(jnp.concatenate([cw[rj][CHUNK - 1:CHUNK] - cw[rj] for rj in rows], axis=0))

    ri = lax.broadcasted_iota(jnp.int32, (p2, p2), 0)
    ci = lax.broadcasted_iota(jnp.int32, (p2, p2), 1)
    bd_mask = (ri // n) == (ci // n)
    ones_bd = bd_mask.astype(BF16)
    t_idx = lax.broadcasted_iota(jnp.int32, (CHUNK, p2), 0)
    s_idx = lax.broadcasted_iota(jnp.int32, (CHUNK, p2), 1) % n
    strict = s_idx < t_idx
    incl = s_idx <= t_idx
    eye = (s_idx == t_idx).astype(F32)
    bd = lambda x: _block_diag(x, bd_mask)

    pairs = range(n_pairs)
    sls = [slice(p * p2, (p + 1) * p2) for p in pairs]
    def segsum(xs):
        tot = _dot_sel2_r(jnp.concatenate(xs, axis=0), ones_bd)
        return [tot[i * CHUNK:(i + 1) * CHUNK] for i in range(len(xs))]

    units = [(j, p) for j in chunks for p in pairs]
    tile = lambda x, u: x[rows[u[0]], sls[u[1]]]
    r, kh, v, kkr = r_ref[...], kh_ref[...], v_ref[...], kkr_ref[...]
    rkk = r * kh * rk_ref[...]
    prep = {}

    def prep_norms():
        sums = segsum([tile(kkr * kkr, u) for u in units] + [tile(rkk, u) for u in units])
        ss = jnp.concatenate([jnp.concatenate(sums[j * n_pairs:(j + 1) * n_pairs], axis=1) for j in chunks], axis=0)
        kk = kkr / jnp.maximum(jnp.sqrt(ss), 1e-12)
        b = kk * a_ref[...]
        prep["bonus"] = sums[len(units):]
        prep["kk_t"], prep["r_t"] = kk * w_exc, r * w_inc
        prep["b_t"], prep["k_t"] = b * w_inv, kh * w_inv
        prep["k_e"], prep["b_e"] = kh * w_end, b * w_end

    def prep_gram():
        prep["bd_v"] = [bd(tile(v, u)).astype(BF16) for u in units]
        prep["lhs"] = [jnp.concatenate([tile(prep["kk_t"], u), tile(prep["r_t"], u)], axis=0).astype(BF16)
                       for u in units]
        prep["g_b"] = [_dot_nt(prep["lhs"][i], bd(tile(prep["b_t"], u))) for i, u in enumerate(units)]
        prep["g_k"] = [_dot_nt(prep["lhs"][i], bd(tile(prep["k_t"], u))) for i, u in enumerate(units)]

    def prep_products():
        g_b, g_k, bd_v = prep["g_b"], prep["g_k"], prep["bd_v"]
        idx = range(len(units))
        prep["rhs0"] = [_dot(jnp.where(strict, g_k[i][:CHUNK], 0.0), bd_v[i]) for i in idx]
        prep["okv"] = [_dot(jnp.where(incl, g_k[i][CHUNK:], 0.0), bd_v[i]) for i in idx]
        npow = [jnp.where(strict, -g_b[i][:CHUNK], 0.0) for i in idx]
        prep["tinv"] = [eye + npow[i] for i in idx]
        prep["npow"] = [_dot(npow[i], bd(npow[i])) for i in idx]

    def prep_double():
        npow, tinv = prep["npow"], prep["tinv"]
        both = [_dot(jnp.concatenate([npow[i], tinv[i]], axis=0), bd(npow[i])) for i in range(len(units))]
        prep["npow"] = [x[:CHUNK] for x in both]
        prep["tinv"] = [tinv[i] + both[i][CHUNK:] for i in range(len(units))]

    def prep_finish():
        npow, tinv = prep["npow"], prep["tinv"]
        prep["tinv"] = [tinv[i] + _dot(tinv[i], bd(npow[i])) for i in range(len(units))]

    state = [st_ref[p] for p in pairs]
    fin = {}

    def fin_from_state(j):
        c = RW_CB * (step - 1) + j
        seq, first, _ = _chunk_pos(jnp.maximum(c, 0), geom)
        first = jnp.logical_and(first, step >= 1)
        fin["seq"] = seq
        fin["st0"] = [jnp.where(first, si_ref[seq, p], state[p]) for p in pairs]
        fin["fs"] = [_dot_nt(lhs_s[rslot, j * n_pairs + p], fin["st0"][p]) for p in pairs]

    def fin_solve(j):
        fin["u"] = [_dot(tinv_s[rslot, j * n_pairs + p], bd(fin["fs"][p][:CHUNK] + rhs0_s[rslot, j * n_pairs + p]))
                    for p in pairs]

    def fin_out_state(j):
        fin["o"] = [fin["fs"][p][CHUNK:] + okv_s[rslot, j * n_pairs + p]
                    - _dot(brb_s[rslot, j * n_pairs + p], bd(fin["u"][p])) for p in pairs]
        for p in pairs:
            lhs_upd = jnp.concatenate([vb_s[rslot, rows[j], sls[p]], (-fin["u"][p]).astype(BF16)], axis=0)
            upd = _dot_tn(lhs_upd, keb_s[rslot, j * n_pairs + p])
            state[p] = fin["st0"][p] * decay_s[rslot, j, :, sls[p]] + jnp.where(bd_mask, upd, 0.0)
            so_ref[fin["seq"], p] = state[p]

    def fin_mean(j):
        mean = segsum(fin["o"])
        fin["oc"] = [fin["o"][p] - mean[p] * (1.0 / n) for p in pairs]

    def fin_write(j):
        var = segsum([x * x for x in fin["oc"]])
        for p in pairs:
            s = sls[p]
            on = fin["oc"][p] * lax.rsqrt(var[p] * (1.0 / n) + RW_GN_EPS) * lg_ref[:, s] + lb_ref[:, s]
            y_ref[rows[j], s] = (on + bonus_s[rslot, rows[j], s]) * g_ref[rows[j], s]

    prep_stages = [prep_norms, prep_gram, prep_products] + [prep_double] * 4 + [prep_finish]
    fin_stages = [functools.partial(f, j) for j in chunks
                  for f in (fin_from_state, fin_solve, fin_out_state, fin_mean, fin_write)]
    for k in range(max(len(prep_stages), len(fin_stages))):
        if k < len(fin_stages):
            fin_stages[k]()
        if k < len(prep_stages):
            prep_stages[k]()
    for p in pairs:
        st_ref[p] = state[p]

    for i, u in enumerate(units):
        lhs_s[wslot, i] = prep["lhs"][i]
        tinv_s[wslot, i] = prep["tinv"][i].astype(BF16)
        brb_s[wslot, i] = jnp.where(incl, prep["g_b"][i][CHUNK:], 0.0).astype(BF16)
        keb_s[wslot, i] = jnp.concatenate([tile(prep["k_e"], u), tile(prep["b_e"], u)], axis=0).astype(BF16)
        rhs0_s[wslot, i] = prep["rhs0"][i]
        okv_s[wslot, i] = prep["okv"][i]
        bonus_s[wslot, rows[u[0]], sls[u[1]]] = prep["bonus"][i] * tile(v, u)
    vb_s[wslot] = v.astype(BF16)
    for j in chunks:
        decay_s[wslot, j] = w_inc[rows[j]][CHUNK - 1:CHUNK]


def _rwkv(r, lw, kh, v, kkr, a, g, rk, lg, lb, s_in, geom: Geom):
    t, d = r.shape
    n = RW_N
    p2 = 2 * n
    n_pairs = d // p2
    n_seq = s_in.shape[0]
    assert geom.n_chunks % RW_CB == 0
    n_blocks = geom.n_chunks // RW_CB
    rb = RW_CB * CHUNK
    s4 = s_in.reshape(n_seq, n_pairs, 2, n, n)
    zero = jnp.zeros_like(s4[:, :, 0])
    s_bd = jnp.concatenate([jnp.concatenate([s4[:, :, 0], zero], axis=-1),
                            jnp.concatenate([zero, s4[:, :, 1]], axis=-1)], axis=-2)
    nxt = pl.BlockSpec((rb, d), lambda s: (jnp.minimum(s, n_blocks - 1), 0))
    cur = pl.BlockSpec((rb, d), lambda s: (jnp.maximum(s - 1, 0), 0))
    vec = _const_spec((1, d))
    per_unit = lambda rows, dtype: pltpu.VMEM((2, RW_CB * n_pairs, rows, p2), dtype)
    y, so = pl.pallas_call(
        functools.partial(_rwkv_kernel, geom=geom, n_pairs=n_pairs),
        grid=(n_blocks + 1,),
        in_specs=[nxt] * 6 + [cur] + [vec] * 3 + [_const_spec(s_bd.shape)],
        out_specs=[cur, _const_spec(s_bd.shape)],
        out_shape=[jax.ShapeDtypeStruct((t, d), F32), jax.ShapeDtypeStruct(s_bd.shape, F32)],
        scratch_shapes=[pltpu.VMEM((n_pairs, p2, p2), F32),
                        per_unit(p2, BF16), per_unit(CHUNK, BF16), per_unit(CHUNK, BF16), per_unit(p2, BF16),
                        pltpu.VMEM((2, rb, d), BF16), per_unit(CHUNK, F32), per_unit(CHUNK, F32),
                        pltpu.VMEM((2, rb, d), F32), pltpu.VMEM((2, RW_CB, 1, d), F32)],
        compiler_params=_params(),
        name="rwkv_scan",
    )(r, lw, kh, v, kkr, a, g, rk.reshape(1, d), lg.reshape(1, d), lb.reshape(1, d), s_bd)
    s_out = jnp.stack([so[:, :, :n, :n], so[:, :, n:, n:]], axis=2).reshape(s_in.shape)
    return y, s_out


def _rwkv_layer(x, gn, s_in, shift_in, p, j, geom):
    d = x[0].shape[1]
    seq_first = _first_chunks(geom)
    start_rows = jnp.zeros((geom.n_chunks, d), F32).at[seq_first].set(shift_in)
    r, lw, kh, v, kkr, a, g, hl = _rwkv_proj(x, gn, start_rows, p, j, geom)
    y, s_out = _rwkv(r, lw, kh, v, kkr, a, g, p["rw_rk"][j], p["rw_ln_g"][j], p["rw_ln_b"][j], s_in, geom)
    return y, s_out, hl.reshape(geom.n_chunks, d)[_last_chunks(geom)]


def _first_chunks(g: Geom):
    return np.concatenate([np.arange(g.nb_p) * g.cps_p,
                           g.npc + np.arange(g.n_seq - g.nb_p) * g.cps_s])


def _last_chunks(g: Geom):
    return _first_chunks(g) + np.concatenate([np.full(g.nb_p, g.cps_p - 1),
                                              np.full(g.n_seq - g.nb_p, g.cps_s - 1)])


def _gelu_tanh(x):
    return 0.5 * x * (1.0 + jnp.tanh(0.7978845608028654 * (x + 0.044715 * x * x * x)))


def _lru_proj_kernel(*refs, n_x, np_tiles):
    x_refs, (gn_ref, wy_ref, wx_ref, y_ref, ux_ref) = refs[:n_x], refs[n_x:]
    h = _rms(_slab_tile(x_refs, np_tiles), gn_ref[...]).astype(BF16)
    y_ref[...] = _gelu_tanh(jnp.dot(h, wy_ref[...], preferred_element_type=F32))
    ux_ref[...] = jnp.dot(h, wx_ref[...], preferred_element_type=F32)


def _lru_proj(x, gn, wy, wx, *, tm=512):
    d = x[0].shape[1]
    rows, np_tiles, n_tiles = _slab_specs([a.shape for a in x], tm)
    out = jax.ShapeDtypeStruct((n_tiles * tm, wy.shape[1]), F32)
    return pl.pallas_call(
        functools.partial(_lru_proj_kernel, n_x=len(x), np_tiles=np_tiles),
        grid=(n_tiles,),
        in_specs=rows + [_const_spec((1, d)), _const_spec(wy.shape), _const_spec(wx.shape)],
        out_specs=[pl.BlockSpec((tm, wy.shape[1]), lambda i: (i, 0))] * 2,
        out_shape=[out, out],
        compiler_params=_params("parallel"),
        name="lru_proj",
    )(*x, gn.reshape(1, d), wy.astype(BF16), wx.astype(BF16))


def _shift_rows(x, d, fill, row):
    return jnp.where(row < d, fill, pltpu.roll(x, d, 0))


def _lru_kernel(ux_ref, y_ref, cw_ref, cb_ref, gaw_ref, gab_ref, gxw_ref, gxb_ref, lam_ref,
                hi_ref, bi_ref, hy_ref, ho_ref, bo_ref, win_ref, carry_ref, hc_ref, *, geom, conv_k):
    step = pl.program_id(0)
    top = SUBLANES
    tail = conv_k - 1
    chunks = range(CONV_CB)
    rows = [slice(j * CHUNK, (j + 1) * CHUNK) for j in chunks]

    @pl.when(step == 0)
    def _():
        carry_ref[...] = jnp.zeros(carry_ref.shape, F32)
        hc_ref[...] = jnp.zeros(hc_ref.shape, F32)

    pos = [_chunk_pos(step * CONV_CB + j, geom) for j in chunks]
    for j in chunks:
        seq, first, _ = pos[j]
        before = carry_ref[...] if j == 0 else ux_ref[j * CHUNK - top:j * CHUNK]
        win_ref[j, :top] = jnp.where(first, bi_ref[seq], before)
        win_ref[j, top:] = ux_ref[rows[j]]
        _window_conv(win_ref.at[j], cw_ref, cb_ref, hy_ref, top - tail, conv_k, out_row=j * CHUNK)
        bo_ref[seq] = ux_ref[(j + 1) * CHUNK - top:(j + 1) * CHUNK]
    carry_ref[...] = ux_ref[CONV_CB * CHUNK - top:]
    u = hy_ref[...]

    bw = u.shape[1] // LRU_BLOCKS
    zr, zi = [], []
    for blk in range(LRU_BLOCKS):
        ub = u[:, blk * bw:(blk + 1) * bw].astype(BF16)
        zr.append(jnp.dot(ub, gaw_ref[blk], preferred_element_type=F32))
        zi.append(jnp.dot(ub, gxw_ref[blk], preferred_element_type=F32))
    r = _sigmoid(jnp.concatenate(zr, axis=1) + gab_ref[...])
    i = _sigmoid(jnp.concatenate(zi, axis=1) + gxb_ref[...])
    nl = -lam_ref[...]
    softplus = jnp.maximum(nl, 0.0) + jnp.log1p(jnp.exp(-jnp.abs(nl)))
    log_a = -LRU_C * r * softplus
    a = jnp.exp(log_a)
    bterm = jnp.sqrt(-jnp.tanh(log_a) * (a * a + 1.0)) * (i * u)

    row = lax.broadcasted_iota(jnp.int32, a.shape, 0) % CHUNK
    d = 1
    while d < CHUNK:
        a_s = _shift_rows(a, d, 1.0, row)
        b_s = _shift_rows(bterm, d, 0.0, row)
        bterm = bterm + a * b_s
        a = a * a_s
        d *= 2
    h_prev = hc_ref[...]
    for j in chunks:
        seq, first, _ = pos[j]
        h = a[rows[j]] * jnp.where(first, hi_ref[seq], h_prev) + bterm[rows[j]]
        hy_ref[rows[j]] = h * y_ref[rows[j]]
        h_prev = h[CHUNK - 1:CHUNK]
        ho_ref[seq] = h_prev
    hc_ref[...] = h_prev


def _lru(ux, y, h_in, buf_in, p, j, geom: Geom):
    t, w = ux.shape
    conv_k = p["lru_conv_w"].shape[1]
    tail = conv_k - 1
    assert geom.n_chunks % CONV_CB == 0 and tail <= SUBLANES
    row = pl.BlockSpec((CONV_CB * CHUNK, w), lambda c: (c, 0))
    vec = _const_spec((1, w))
    gspec = _const_spec(p["lru_ga_w"].shape[1:])
    h3 = h_in.reshape(geom.n_seq, 1, w)
    b8 = jnp.pad(buf_in, ((0, 0), (SUBLANES - tail, 0), (0, 0)))
    hy, ho, bo = pl.pallas_call(
        functools.partial(_lru_kernel, geom=geom, conv_k=conv_k),
        grid=(geom.n_chunks // CONV_CB,),
        in_specs=[row, row, _const_spec((conv_k, w)), vec, gspec, vec, gspec, vec, vec,
                  _const_spec(h3.shape), _const_spec(b8.shape)],
        out_specs=[row, _const_spec(h3.shape), _const_spec(b8.shape)],
        out_shape=[jax.ShapeDtypeStruct((t, w), F32), jax.ShapeDtypeStruct(h3.shape, F32),
                   jax.ShapeDtypeStruct(b8.shape, F32)],
        scratch_shapes=[pltpu.VMEM((CONV_CB, CHUNK + SUBLANES, w), F32), pltpu.VMEM((SUBLANES, w), F32),
                        pltpu.VMEM((1, w), F32)],
        compiler_params=_params(),
        name="lru_scan",
    )(ux, y, p["lru_conv_w"][j], p["lru_conv_b"][j].reshape(1, w), p["lru_ga_w"][j].astype(BF16),
      p["lru_ga_b"][j].reshape(1, w), p["lru_gx_w"][j].astype(BF16), p["lru_gx_b"][j].reshape(1, w),
      p["lru_lam"][j].reshape(1, w), h3, b8)
    return hy, ho, bo[:, SUBLANES - tail:]


def _lru_layer(x, gn, h_in, buf_in, p, j, geom):
    y, ux = _lru_proj(x, gn, p["lru_wy"][j], p["lru_wx"][j])
    hy, h_out, buf_out = _lru(ux, y, h_in, buf_in, p, j, geom)
    return hy, h_out.reshape(h_in.shape), buf_out


def _conf_proj_kernel(*refs, n_x, np_tiles):
    x_refs, (gn_ref, w1_ref, b1_ref, u_ref) = refs[:n_x], refs[n_x:]
    d = u_ref.shape[1]
    h = _rms(_slab_tile(x_refs, np_tiles), gn_ref[...]).astype(BF16)
    hh = jnp.dot(h, w1_ref[...], preferred_element_type=F32) + b1_ref[...]
    u_ref[...] = hh[:, :d] * _sigmoid(hh[:, d:])


def _conf_proj(x, gn, w1, b1, *, tm=512):
    d = x[0].shape[1]
    rows, np_tiles, n_tiles = _slab_specs([a.shape for a in x], tm)
    return pl.pallas_call(
        functools.partial(_conf_proj_kernel, n_x=len(x), np_tiles=np_tiles),
        grid=(n_tiles,),
        in_specs=rows + [_const_spec((1, d)), _const_spec(w1.shape), _const_spec((1, w1.shape[1]))],
        out_specs=pl.BlockSpec((tm, d), lambda i: (i, 0)),
        out_shape=jax.ShapeDtypeStruct((n_tiles * tm, d), F32),
        compiler_params=_params("parallel"),
        name="conf_proj",
    )(*x, gn.reshape(1, d), w1.astype(BF16), b1.reshape(1, -1))


def _window_conv(win_ref, w_ref, b_ref, out_ref, base, conv_k, out_row=0):
    for lt in range(out_ref.shape[1] // LANES):
        sl = slice(lt * LANES, (lt + 1) * LANES)
        acc = jnp.broadcast_to(b_ref[:, sl], (CHUNK, LANES))
        for b in range(SUBLANES):
            offs = [o for o in range(base, base + conv_k) if o % SUBLANES == b]
            if not offs:
                continue
            rows = CHUNK + SUBLANES if b else CHUNK
            part = sum(w_ref[o - base:o - base + 1, sl] * win_ref[o - b:o - b + rows, sl] for o in offs)
            acc = acc + (pltpu.roll(part, rows - b, 0)[:CHUNK] if b else part)
        out_ref[out_row:out_row + CHUNK, sl] = acc


def _conf_kernel(u_ref, w_ref, b_ref, lg_ref, lb_ref, bi_ref, c_ref, bo_ref, win_ref, carry_ref,
                 *, geom, conv_k, top):
    step = pl.program_id(0)
    tail = conv_k - 1

    @pl.when(step == 0)
    def _():
        carry_ref[...] = jnp.zeros(carry_ref.shape, F32)

    for j in range(CONV_CB):
        seq, first, _ = _chunk_pos(step * CONV_CB + j, geom)
        before = carry_ref[...] if j == 0 else u_ref[j * CHUNK - top:j * CHUNK]
        win_ref[j, :top] = jnp.where(first, bi_ref[seq], before)
        win_ref[j, top:] = u_ref[j * CHUNK:(j + 1) * CHUNK]
        _window_conv(win_ref.at[j], w_ref, b_ref, c_ref, top - tail, conv_k, out_row=j * CHUNK)
        bo_ref[seq] = u_ref[(j + 1) * CHUNK - top:(j + 1) * CHUNK]
    carry_ref[...] = u_ref[CONV_CB * CHUNK - top:]
    acc = c_ref[...]
    mu = jnp.mean(acc, axis=-1, keepdims=True)
    xc = acc - mu
    var = jnp.mean(xc * xc, axis=-1, keepdims=True)
    c_ref[...] = _silu(xc * lax.rsqrt(var + EPS) * lg_ref[...] + lb_ref[...])


def _conf(u, buf_in, p, j, geom: Geom):
    t, d = u.shape
    conv_k = p["cf_dw_w"].shape[1]
    tail = conv_k - 1
    top = -(-tail // SUBLANES) * SUBLANES
    assert top <= CHUNK and geom.n_chunks % CONV_CB == 0
    row = pl.BlockSpec((CONV_CB * CHUNK, d), lambda c: (c, 0))
    vec = _const_spec((1, d))
    b_top = jnp.pad(buf_in, ((0, 0), (top - tail, 0), (0, 0)))
    c, bo = pl.pallas_call(
        functools.partial(_conf_kernel, geom=geom, conv_k=conv_k, top=top),
        grid=(geom.n_chunks // CONV_CB,),
        in_specs=[row, _const_spec((conv_k, d)), vec, vec, vec, _const_spec(b_top.shape)],
        out_specs=[row, _const_spec(b_top.shape)],
        out_shape=[jax.ShapeDtypeStruct((t, d), F32), jax.ShapeDtypeStruct(b_top.shape, F32)],
        scratch_shapes=[pltpu.VMEM((CONV_CB, CHUNK + top, d), F32), pltpu.VMEM((top, d), F32)],
        compiler_params=_params(),
        name="conf_conv",
    )(u, p["cf_dw_w"][j], p["cf_dw_b"][j].reshape(1, d), p["cf_ln_g"][j].reshape(1, d),
      p["cf_ln_b"][j].reshape(1, d), b_top)
    return c, bo[:, top - tail:]


def _conf_layer(x, gn, buf_in, p, j, geom):
    u = _conf_proj(x, gn, p["cf_w1"][j], p["cf_b1"][j])
    return _conf(u, buf_in, p, j, geom)


_ARG_NAMES = (
    "x_prompt x_sample state_hgrn state_rwkv state_rwkv_shift state_lru state_lru_conv state_conf_conv "
    "norm_mix norm_ffn norm_final hg_wq hg_wf hg_wi hg_wg hg_gn hg_wo hg_lb rw_mu rw_wr rw_wk rw_wv rw_w0 "
    "rw_w1 rw_w2 rw_a0 rw_a1 rw_a2 rw_g1 rw_g2 rw_kk rw_ka rw_rk rw_ln_g rw_ln_b rw_wo lru_wy lru_wx "
    "lru_conv_w lru_conv_b lru_ga_w lru_ga_b lru_gx_w lru_gx_b lru_lam lru_wo cf_w1 cf_b1 cf_dw_w cf_dw_b "
    "cf_ln_g cf_ln_b cf_w2 cf_b2 ffn_w1 ffn_w3 ffn_w2").split()
N_MIXERS = 4


def kernel(x_prompt, x_sample, state_hgrn, state_rwkv, state_rwkv_shift, state_lru, state_lru_conv, state_conf_conv, norm_mix, norm_ffn, norm_final, hg_wq, hg_wf, hg_wi, hg_wg, hg_gn, hg_wo, hg_lb, rw_mu, rw_wr, rw_wk, rw_wv, rw_w0, rw_w1, rw_w2, rw_a0, rw_a1, rw_a2, rw_g1, rw_g2, rw_kk, rw_ka, rw_rk, rw_ln_g, rw_ln_b, rw_wo, lru_wy, lru_wx, lru_conv_w, lru_conv_b, lru_ga_w, lru_ga_b, lru_gx_w, lru_gx_b, lru_lam, lru_wo, cf_w1, cf_b1, cf_dw_w, cf_dw_b, cf_ln_g, cf_ln_b, cf_w2, cf_b2, ffn_w1, ffn_w3, ffn_w2):
    p = dict(zip(_ARG_NAMES, (x_prompt, x_sample, state_hgrn, state_rwkv, state_rwkv_shift, state_lru, state_lru_conv, state_conf_conv, norm_mix, norm_ffn, norm_final, hg_wq, hg_wf, hg_wi, hg_wg, hg_gn, hg_wo, hg_lb, rw_mu, rw_wr, rw_wk, rw_wv, rw_w0, rw_w1, rw_w2, rw_a0, rw_a1, rw_a2, rw_g1, rw_g2, rw_kk, rw_ka, rw_rk, rw_ln_g, rw_ln_b, rw_wo, lru_wy, lru_wx, lru_conv_w, lru_conv_b, lru_ga_w, lru_ga_b, lru_gx_w, lru_gx_b, lru_lam, lru_wo, cf_w1, cf_b1, cf_dw_w, cf_dw_b, cf_ln_g, cf_ln_b, cf_w2, cf_b2, ffn_w1, ffn_w3, ffn_w2)))
    nb_p, seq_p, d = x_prompt.shape
    nb_s, seq_s, _ = x_sample.shape
    assert seq_p % CHUNK == 0 and seq_s % CHUNK == 0
    npc = nb_p * seq_p // CHUNK
    geom = Geom(n_chunks=npc + nb_s * seq_s // CHUNK, npc=npc, cps_p=seq_p // CHUNK,
                cps_s=seq_s // CHUNK, nb_p=nb_p, n_seq=nb_p + nb_s)
    x = (x_prompt.reshape(-1, d), x_sample.reshape(-1, d))
    split_shapes = [a.shape for a in x]
    n_rows = sum(s[0] for s in split_shapes)

    def all_seqs(state):
        return jnp.concatenate([jnp.zeros((nb_p,) + state.shape[1:], state.dtype), state], axis=0)

    depth = norm_mix.shape[0]
    zero_bias = jnp.zeros((d,), F32)
    outs = {k: [] for k in ("hg", "rw", "sh", "lh", "lc", "cf")}
    for i in range(depth):
        m, j = i % N_MIXERS, i // N_MIXERS
        gn = norm_mix[i]
        bo = zero_bias
        if m == 0:
            y, s = _hgrn_layer(x, gn, all_seqs(state_hgrn[j]), p, j, geom)
            outs["hg"].append(s)
            wo = hg_wo[j]
        elif m == 1:
            y, s, sh = _rwkv_layer(x, gn, all_seqs(state_rwkv[j]), all_seqs(state_rwkv_shift[j]), p, j, geom)
            outs["rw"].append(s)
            outs["sh"].append(sh)
            wo = rw_wo[j]
        elif m == 2:
            y, hl, cb = _lru_layer(x, gn, all_seqs(state_lru[j]), all_seqs(state_lru_conv[j]), p, j, geom)
            outs["lh"].append(hl)
            outs["lc"].append(cb)
            wo = lru_wo[j]
        else:
            y, cb = _conf_layer(x, gn, all_seqs(state_conf_conv[j]), p, j, geom)
            outs["cf"].append(cb)
            wo, bo = cf_w2[j], cf_b2[j]
        final = i == depth - 1
        out_shapes = split_shapes if final else [(n_rows, d)]
        x = _post_ffn(x, y, wo, bo, norm_ffn[i], ffn_w1[i], ffn_w3[i], ffn_w2[i], norm_final,
                      final=final, out_shapes=out_shapes)

    stacked = [jnp.stack(outs[k]) for k in ("hg", "rw", "sh", "lh", "lc", "cf")]
    return ((x[0].reshape(nb_p, seq_p, d), x[1].reshape(nb_s, seq_s, d))
            + tuple(s[:, :nb_p] for s in stacked) + tuple(s[:, nb_p:] for s in stacked))
```

```python
import functools
from typing import NamedTuple

import numpy as np
import jax
import jax.numpy as jnp
from jax import lax
from jax.experimental import pallas as pl
from jax.experimental.pallas import tpu as pltpu

F32 = jnp.float32
BF16 = jnp.bfloat16

EPS = 1e-6
CHUNK = 64
SUB = 16
LANES = 128
SUBLANES = 8
HG_DK = 128
RW_N = 64
RW_CB = 2
HG_CB = 2
CONV_CB = 4
RW_DECAY_SCALE = 0.6065306597126334
RW_GN_EPS = 64e-5
LRU_BLOCKS = 8
LRU_C = 8.0
VMEM_LIMIT = 56 * 1024 * 1024


class Geom(NamedTuple):
    n_chunks: int
    npc: int
    cps_p: int
    cps_s: int
    nb_p: int
    n_seq: int


def _chunk_pos(c, g: Geom):
    in_p = c < g.npc
    cs = c - g.npc
    seq = jnp.where(in_p, c // g.cps_p, g.nb_p + cs // g.cps_s)
    first = jnp.where(in_p, c % g.cps_p == 0, cs % g.cps_s == 0)
    last = jnp.where(in_p, c % g.cps_p == g.cps_p - 1, cs % g.cps_s == g.cps_s - 1)
    return seq, first, last


def _dot(a, b):
    return jnp.dot(a.astype(BF16), b.astype(BF16), preferred_element_type=F32)


def _dot_nt(a, b):
    return lax.dot_general(a.astype(BF16), b.astype(BF16), (((1,), (1,)), ((), ())),
                           preferred_element_type=F32)


def _dot_tn(a, b):
    return lax.dot_general(a.astype(BF16), b.astype(BF16), (((0,), (0,)), ((), ())),
                           preferred_element_type=F32)


def _split3(x):
    hi = x.astype(BF16)
    r1 = x - hi.astype(F32)
    mid = r1.astype(BF16)
    lo = (r1 - mid.astype(F32)).astype(BF16)
    return hi, mid, lo


def _dot_sel(sel_bf16, x):
    hi, mid, lo = _split3(x)
    d = lambda t: jnp.dot(sel_bf16, t, preferred_element_type=F32)
    return d(hi) + d(mid) + d(lo)


def _dot_sel2_r(x, sel_bf16):
    hi = x.astype(BF16)
    lo = (x - hi.astype(F32)).astype(BF16)
    return (jnp.dot(hi, sel_bf16, preferred_element_type=F32)
            + jnp.dot(lo, sel_bf16, preferred_element_type=F32))


def _rms(x, g):
    return x * lax.rsqrt(jnp.mean(x * x, axis=-1, keepdims=True) + EPS) * g


def _sigmoid(x):
    return 0.5 * jnp.tanh(0.5 * x) + 0.5


def _silu(x, scale=1.0):
    return (0.5 * scale) * x * (jnp.tanh(0.5 * x) + 1.0)


def _tri_incl(n):
    r = lax.broadcasted_iota(jnp.int32, (n, n), 0)
    c = lax.broadcasted_iota(jnp.int32, (n, n), 1)
    return (r >= c).astype(BF16)


def _const_spec(shape):
    return pl.BlockSpec(shape, lambda *_: (0,) * len(shape), pipeline_mode=pl.Buffered(1))


def _slab_specs(shapes, tm):
    d = shapes[0][1]
    assert all(s[0] % tm == 0 for s in shapes)
    if len(shapes) == 1:
        n_tiles = shapes[0][0] // tm
        return [pl.BlockSpec((tm, d), lambda i: (i, 0))], n_tiles, n_tiles
    np_tiles = shapes[0][0] // tm
    specs = [pl.BlockSpec((tm, d), lambda i: (jnp.minimum(i, np_tiles - 1), 0)),
             pl.BlockSpec((tm, d), lambda i: (jnp.maximum(i - np_tiles, 0), 0))]
    return specs, np_tiles, np_tiles + shapes[1][0] // tm


def _slab_tile(x_refs, np_tiles):
    if len(x_refs) == 1:
        return x_refs[0][...]
    return jnp.where(pl.program_id(0) < np_tiles, x_refs[0][...], x_refs[1][...])


def _params(sem="arbitrary"):
    return pltpu.CompilerParams(dimension_semantics=(sem,), vmem_limit_bytes=VMEM_LIMIT)


def _post_ffn_kernel(*refs, n_x, final, np_in, np_out):
    x_refs, (y_ref, wo_ref, bo_ref, gn_ref, w1_ref, w3_ref, w2_ref, gf_ref) = refs[:n_x], refs[n_x:n_x + 8]
    o_refs = refs[n_x + 8:]
    x1 = _slab_tile(x_refs, np_in) + _dot(y_ref[...], wo_ref[...]) + bo_ref[...]
    h = _rms(x1, gn_ref[...]).astype(BF16)
    a = jnp.dot(h, w1_ref[...], preferred_element_type=F32)
    b = jnp.dot(h, w3_ref[...], preferred_element_type=F32)
    out = x1 + _dot(_silu(a) * b, w2_ref[...])
    if final:
        out = _rms(out, gf_ref[...])
    if len(o_refs) == 1:
        o_refs[0][...] = out
    else:
        in_first = pl.program_id(0) < np_out

        @pl.when(in_first)
        def _():
            o_refs[0][...] = out

        @pl.when(jnp.logical_not(in_first))
        def _():
            o_refs[1][...] = out


def _post_ffn(x, y, wo, bo, gn, w1, w3, w2, gf, *, final, out_shapes, tm=512):
    d = x[0].shape[1]
    rows, np_in, n_tiles = _slab_specs([a.shape for a in x], tm)
    out_rows, np_out, _ = _slab_specs(out_shapes, tm)
    return pl.pallas_call(
        functools.partial(_post_ffn_kernel, n_x=len(x), final=final, np_in=np_in, np_out=np_out),
        grid=(n_tiles,),
        in_specs=rows + [pl.BlockSpec((tm, y.shape[1]), lambda i: (i, 0)),
                         _const_spec(wo.shape), _const_spec((1, d)), _const_spec((1, d)),
                         _const_spec(w1.shape), _const_spec(w3.shape), _const_spec(w2.shape),
                         _const_spec((1, d))],
        out_specs=out_rows,
        out_shape=[jax.ShapeDtypeStruct(s, F32) for s in out_shapes],
        compiler_params=_params(),
        name="post_ffn",
    )(*x, y, wo.astype(BF16), bo.reshape(1, d), gn.reshape(1, d), w1.astype(BF16), w3.astype(BF16),
      w2.astype(BF16), gf.reshape(1, d))


def _hgrn_proj_kernel(*refs, n_x, layer, np_tiles):
    x_refs = refs[:n_x]
    (gn_ref, lbp_ref, wq_ref, wf_ref, wi_ref, wg_ref, q_ref, lk_ref, lf_ref, v_ref, gate_ref) = refs[n_x:]
    h = _rms(_slab_tile(x_refs, np_tiles), gn_ref[...]).astype(BF16)
    lbp = lbp_ref[...]
    e = jnp.exp(lbp - jnp.max(lbp, axis=0, keepdims=True))
    lb = jnp.sum(e[:layer + 1], axis=0, keepdims=True) / jnp.sum(e, axis=0, keepdims=True)
    q_ref[...] = _silu(jnp.dot(h, wq_ref[...], preferred_element_type=F32), HG_DK ** -0.5)
    fl = jnp.dot(h, wf_ref[...], preferred_element_type=F32)
    l1p = jnp.log1p(jnp.exp(-jnp.abs(fl)))
    lk_ref[...] = jnp.log(1.0 - lb) - (jnp.maximum(fl, 0.0) + l1p)
    sig = jnp.exp(jnp.minimum(fl, 0.0) - l1p)
    lf_ref[...] = jnp.log(lb + (1.0 - lb) * sig)
    v_ref[...] = jnp.dot(h, wi_ref[...], preferred_element_type=F32)
    gate_ref[...] = _silu(jnp.dot(h, wg_ref[...], preferred_element_type=F32))


def _hgrn_proj(x, gn, lbp, wq, wf, wi, wg, *, layer, tm=512):
    d = x[0].shape[1]
    rows, np_tiles, n_tiles = _slab_specs([a.shape for a in x], tm)
    row = pl.BlockSpec((tm, d), lambda i: (i, 0))
    wspec = _const_spec((d, d))
    out = jax.ShapeDtypeStruct((n_tiles * tm, d), F32)
    return pl.pallas_call(
        functools.partial(_hgrn_proj_kernel, n_x=len(x), layer=layer, np_tiles=np_tiles),
        grid=(n_tiles,),
        in_specs=rows + [_const_spec((1, d)), _const_spec(lbp.shape), wspec, wspec, wspec, wspec],
        out_specs=[row] * 5,
        out_shape=[out] * 5,
        compiler_params=_params("parallel"),
        name="hgrn_proj",
    )(*x, gn.reshape(1, d), lbp, wq.astype(BF16), wf.astype(BF16), wi.astype(BF16), wg.astype(BF16))


def _gla_kernel(q_ref, lk_ref, lf_ref, v_ref, gate_ref, gn_ref, si_ref, og_ref, so_ref,
                st_ref, zp_ref, *, geom, n_heads):
    step = pl.program_id(0)
    chunks = range(HG_CB)
    rows = [slice(j * CHUNK, (j + 1) * CHUNK) for j in chunks]

    @pl.when(step == 0)
    def _():
        st_ref[...] = jnp.zeros(st_ref.shape, F32)

    tri = _tri_incl(CHUNK)
    lf_all = lf_ref[...]
    cum = jnp.concatenate([_dot_sel(tri, lf_all[rj]) for rj in rows], axis=0)
    z_all = lk_ref[...] - cum
    for j in chunks:
        zp_ref[j, :SUB] = jnp.zeros((SUB, zp_ref.shape[2]), F32)
        zp_ref[j, SUB:] = z_all[rows[j]]
    ones = jnp.ones((HG_DK, LANES), BF16)
    t_idx = lax.broadcasted_iota(jnp.int32, (CHUNK, LANES), 0)
    s_idx = lax.broadcasted_iota(jnp.int32, (CHUNK, LANES), 1)
    pair_delta = jnp.where((s_idx <= t_idx) & (s_idx // SUB == t_idx // SUB), t_idx - s_idx, -1)
    n_sub = CHUNK // SUB
    batch = 4

    heads = range(n_heads)
    sls = [slice(h * HG_DK, (h + 1) * HG_DK) for h in heads]
    units = [(j, h) for j in chunks for h in heads]
    tile = lambda x, u: x[rows[u[0]], sls[u[1]]]
    q_all, v_all = q_ref[...], v_ref[...]
    last_cum = jnp.concatenate([jnp.broadcast_to(cum[rj][CHUNK - 1:CHUNK], (CHUNK, cum.shape[1])) for rj in rows],
                               axis=0)
    q_in = q_all * jnp.exp(cum)
    k_out = jnp.exp(z_all + last_cum)
    upd = [_dot_tn(tile(v_all, u), tile(k_out, u)) for u in units]
    blocks = [[jnp.zeros((SUB, HG_DK), F32)] for _ in units]
    for i in range(1, n_sub):
        lo, hi = i * SUB, (i + 1) * SUB
        sc = []
        for j, h in units:
            cj, zj, qj = cum[rows[j]][:, sls[h]], z_all[rows[j]][:, sls[h]], q_all[rows[j]][:, sls[h]]
            ref = cj[lo - 1:lo]
            sc.append(_dot_nt(qj[lo:hi] * jnp.exp(cj[lo:hi] - ref), jnp.exp(zj[:lo] + ref)))
        for i_u, u in enumerate(units):
            blocks[i_u].append(_dot(sc[i_u], tile(v_all, u)[:lo]))
    half = CHUNK // 2
    groups = lambda x, g: jnp.concatenate([x[r:r + SUBLANES] for r in range(g * SUBLANES, x.shape[0], SUB)], axis=0)
    pd = [groups(pair_delta, 0), groups(pair_delta, 1)]
    scores = [[jnp.zeros((half, LANES), F32), jnp.zeros((half, LANES), F32)] for _ in units]
    for b0 in range(0, SUBLANES, batch):
        sums = []
        for j, h in units:
            parts = []
            qu, cu = tile(q_all, (j, h)), tile(cum, (j, h))
            for b in range(b0, b0 + batch):
                zb = zp_ref[j, SUBLANES - b:SUBLANES - b + CHUNK + SUBLANES, sls[h]]
                parts.append((qu * jnp.exp(cu + zb[SUBLANES:])).astype(BF16))
                parts.append((groups(qu, 1) * jnp.exp(groups(cu, 1) + groups(zb[:CHUNK], 1))).astype(BF16))
            sums.append(jnp.dot(jnp.concatenate(parts, axis=0), ones, preferred_element_type=F32))
        for i_u in range(len(units)):
            for i, b in enumerate(range(b0, b0 + batch)):
                base = i * (CHUNK + half)
                full, second = sums[i_u][base:base + CHUNK], sums[i_u][base + CHUNK:base + CHUNK + half]
                for g in range(2):
                    scores[i_u][g] = jnp.where(pd[g] == b, groups(full, g), scores[i_u][g])
                scores[i_u][1] = jnp.where(pd[1] == b + SUBLANES, second, scores[i_u][1])

    def interleave(s):
        return jnp.concatenate([s[g][r:r + SUBLANES] for r in range(0, half, SUBLANES) for g in range(2)], axis=0)

    o_local = [jnp.concatenate(blocks[i_u], axis=0) + _dot(interleave(scores[i_u])[:, :CHUNK], tile(v_all, u))
               for i_u, u in enumerate(units)]

    state = [st_ref[h] for h in heads]
    decay = jnp.exp(last_cum)
    for j in chunks:
        seq, first, _ = _chunk_pos(step * HG_CB + j, geom)
        for h in heads:
            st0 = jnp.where(first, si_ref[seq, h], state[h])
            o = o_local[j * n_heads + h] + _dot_nt(tile(q_in, (j, h)), st0)
            state[h] = st0 * tile(decay, (j, h))[:1] + upd[j * n_heads + h]
            so_ref[seq, h] = state[h]
            on = o * lax.rsqrt(jnp.mean(o * o, axis=-1, keepdims=True) + EPS) * gn_ref[...]
            og_ref[rows[j], sls[h]] = on * gate_ref[rows[j], sls[h]]
    for h in heads:
        st_ref[h] = state[h]


def _gla(q, lk, lf, v, gate, gn, s_in, geom: Geom):
    t, d = q.shape
    n_heads = d // HG_DK
    assert geom.n_chunks % HG_CB == 0
    rb = HG_CB * CHUNK
    row = pl.BlockSpec((rb, d), lambda c: (c, 0))
    s_t = jnp.swapaxes(s_in, -1, -2)
    og, so = pl.pallas_call(
        functools.partial(_gla_kernel, geom=geom, n_heads=n_heads),
        grid=(geom.n_chunks // HG_CB,),
        in_specs=[row] * 5 + [_const_spec((1, HG_DK)), _const_spec(s_t.shape)],
        out_specs=[row, _const_spec(s_t.shape)],
        out_shape=[jax.ShapeDtypeStruct((t, d), F32), jax.ShapeDtypeStruct(s_t.shape, F32)],
        scratch_shapes=[pltpu.VMEM((n_heads, HG_DK, HG_DK), F32), pltpu.VMEM((HG_CB, CHUNK + SUB, d), F32)],
        compiler_params=_params(),
        name="hgrn_gla",
    )(q, lk, lf, v, gate, gn.reshape(1, HG_DK), s_t)
    return og, jnp.swapaxes(so, -1, -2)


def _hgrn_layer(x, gn, s_in, p, j, geom):
    q, lk, lf, v, gate = _hgrn_proj(x, gn, p["hg_lb"], p["hg_wq"][j], p["hg_wf"][j],
                                    p["hg_wi"][j], p["hg_wg"][j], layer=j)
    return _gla(q, lk, lf, v, gate, p["hg_gn"][j], s_in, geom)


def _rwkv_proj_kernel(*refs, n_x, geom, tm, np_tiles):
    x_refs, prev_refs = refs[:n_x], refs[n_x:2 * n_x]
    (st_ref, gn_ref, mu_ref, wr_ref, wk_ref, wv_ref, w0_ref, w1_ref, w2_ref, a0_ref, a1_ref, a2_ref,
     g1_ref, g2_ref, kk_ref, ka_ref,
     r_ref, lw_ref, kh_ref, v_ref, kkr_ref, a_ref, g_ref, hl_ref, win_ref) = refs[2 * n_x:]
    i = pl.program_id(0)
    cpt = tm // CHUNK
    gn = gn_ref[...]
    h = _rms(_slab_tile(x_refs, np_tiles), gn)
    win_ref[SUBLANES:] = h
    win_ref[:SUBLANES] = _rms(_slab_tile(prev_refs, np_tiles), gn)
    for j in range(cpt):
        last_row = SUBLANES + (j + 1) * CHUNK - 1
        hl_ref[j] = win_ref[last_row:last_row + 1]
    for j in range(cpt):
        _, first, _ = _chunk_pos(i * cpt + j, geom)

        @pl.when(first)
        def _():
            row = SUBLANES - 1 + j * CHUNK
            win_ref[row:row + 1] = st_ref[j]

    xx = win_ref[SUBLANES - 1:SUBLANES - 1 + tm] - h
    mix = lambda n: (h + xx * mu_ref[n:n + 1]).astype(BF16)
    dot = lambda a, w_ref: jnp.dot(a, w_ref[...], preferred_element_type=F32)
    r_ref[...] = dot(mix(0), wr_ref)
    k = dot(mix(2), wk_ref)
    v_ref[...] = dot(mix(3), wv_ref)
    tw = jnp.tanh(dot(mix(1), w1_ref)).astype(BF16)
    lw_ref[...] = -RW_DECAY_SCALE * _sigmoid(w0_ref[...] + dot(tw, w2_ref))
    a = _sigmoid(a0_ref[...] + dot(dot(mix(4), a1_ref).astype(BF16), a2_ref))
    g_ref[...] = dot(_sigmoid(dot(mix(5), g1_ref)).astype(BF16), g2_ref)
    a_ref[...] = a
    kkr_ref[...] = k * kk_ref[...]
    kh_ref[...] = k * (1.0 + (a - 1.0) * ka_ref[...])


def _rwkv_proj(x, gn, start_rows, p, j, geom, *, tm=256):
    d = x[0].shape[1]
    rows, np_tiles, n_tiles = _slab_specs([a.shape for a in x], tm)
    t = n_tiles * tm
    cpt = tm // CHUNK
    row = pl.BlockSpec((tm, d), lambda i: (i, 0))
    bpt = tm // SUBLANES
    prev = [pl.BlockSpec((SUBLANES, d), lambda i: (jnp.clip(i * bpt - 1, 0, np_tiles * bpt - 1), 0))]
    if len(x) == 2:
        prev.append(pl.BlockSpec((SUBLANES, d), lambda i: (jnp.maximum((i - np_tiles) * bpt - 1, 0), 0)))
    vec = _const_spec((1, d))
    bf = lambda w: w.astype(BF16)
    ws = [bf(p["rw_wr"][j]), bf(p["rw_wk"][j]), bf(p["rw_wv"][j]), p["rw_w0"][j].reshape(1, d),
          bf(p["rw_w1"][j]), bf(p["rw_w2"][j]), p["rw_a0"][j].reshape(1, d), bf(p["rw_a1"][j]),
          bf(p["rw_a2"][j]), bf(p["rw_g1"][j]), bf(p["rw_g2"][j]), p["rw_kk"][j].reshape(1, d),
          p["rw_ka"][j].reshape(1, d)]
    out = jax.ShapeDtypeStruct((t, d), F32)
    return pl.pallas_call(
        functools.partial(_rwkv_proj_kernel, n_x=len(x), geom=geom, tm=tm, np_tiles=np_tiles),
        grid=(n_tiles,),
        in_specs=rows + prev + [pl.BlockSpec((cpt, 1, d), lambda i: (i, 0, 0)), vec,
                                _const_spec((6, d))] + [_const_spec(w.shape) for w in ws],
        out_specs=[row] * 7 + [pl.BlockSpec((cpt, 1, d), lambda i: (i, 0, 0))],
        out_shape=[out] * 7 + [jax.ShapeDtypeStruct((geom.n_chunks, 1, d), F32)],
        scratch_shapes=[pltpu.VMEM((tm + SUBLANES, d), F32)],
        compiler_params=_params(),
        name="rwkv_proj",
    )(*x, *x, start_rows.reshape(geom.n_chunks, 1, d), gn.reshape(1, d), p["rw_mu"][j], *ws)


def _block_diag(x, bd_mask):
    return jnp.where(bd_mask, jnp.concatenate([x, x], axis=0), 0.0)


def _rwkv_kernel(r_ref, lw_ref, kh_ref, v_ref, kkr_ref, a_ref, g_ref, rk_ref, lg_ref, lb_ref, si_ref,
                 y_ref, so_ref, st_ref, lhs_s, tinv_s, brb_s, keb_s, vb_s, rhs0_s, okv_s, bonus_s, decay_s,
                 *, geom, n_pairs):
    step = pl.program_id(0)
    wslot = step % 2
    rslot = 1 - wslot
    n = RW_N
    p2 = 2 * n

    @pl.when(step == 0)
    def _():
        for ref in (lhs_s, tinv_s, brb_s, keb_s, vb_s, rhs0_s, okv_s, bonus_s, decay_s):
            ref[1] = jnp.zeros(ref.shape[1:], ref.dtype)
        st_ref[...] = jnp.zeros(st_ref.shape, F32)

    chunks = range(RW_CB)
    rows = [slice(j * CHUNK, (j + 1) * CHUNK) for j in chunks]
    lw = lw_ref[...]
    tri = _tri_incl(CHUNK)
    cw = jnp.concatenate([_dot_sel(tri, lw[rj]) for rj in rows], axis=0)
    w_inc = jnp.exp(cw)
    w_exc = jnp.exp(cw - lw)
    w_inv = jnp.exp(-cw)
    w_end = jnp.exp(jnp.concatenate([cw[rj][CHUNK - 1:CHUNK] - cw[rj] for rj in rows], axis=0))

    ri = lax.broadcasted_iota(jnp.int32, (p2, p2), 0)
    ci = lax.broadcasted_iota(jnp.int32, (p2, p2), 1)
    bd_mask = (ri // n) == (ci // n)
    ones_bd = bd_mask.astype(BF16)
    t_idx = lax.broadcasted_iota(jnp.int32, (CHUNK, p2), 0)
    s_idx = lax.broadcasted_iota(jnp.int32, (CHUNK, p2), 1) % n
    strict = s_idx < t_idx
    incl = s_idx <= t_idx
    eye = (s_idx == t_idx).astype(F32)
    bd = lambda x: _block_diag(x, bd_mask)

    pairs = range(n_pairs)
    sls = [slice(p * p2, (p + 1) * p2) for p in pairs]
    def segsum(xs):
        tot = _dot_sel2_r(jnp.concatenate(xs, axis=0), ones_bd)
        return [tot[i * CHUNK:(i + 1) * CHUNK] for i in range(len(xs))]

    units = [(j, p) for j in chunks for p in pairs]
    tile = lambda x, u: x[rows[u[0]], sls[u[1]]]
    r, kh, v, kkr = r_ref[...], kh_ref[...], v_ref[...], kkr_ref[...]
    rkk = r * kh * rk_ref[...]
    prep = {}

    def prep_norms():
        sums = segsum([tile(kkr * kkr, u) for u in units] + [tile(rkk, u) for u in units])
        ss = jnp.concatenate([jnp.concatenate(sums[j * n_pairs:(j + 1) * n_pairs], axis=1) for j in chunks], axis=0)
        kk = kkr / jnp.maximum(jnp.sqrt(ss), 1e-12)
        b = kk * a_ref[...]
        prep["bonus"] = sums[len(units):]
        prep["kk_t"], prep["r_t"] = kk * w_exc, r * w_inc
        prep["b_t"], prep["k_t"] = b * w_inv, kh * w_inv
        prep["k_e"], prep["b_e"] = kh * w_end, b * w_end

    def prep_gram():
        prep["bd_v"] = [bd(tile(v, u)).astype(BF16) for u in units]
        prep["lhs"] = [jnp.concatenate([tile(prep["kk_t"], u), tile(prep["r_t"], u)], axis=0).astype(BF16)
                       for u in units]
        prep["g_b"] = [_dot_nt(prep["lhs"][i], bd(tile(prep["b_t"], u))) for i, u in enumerate(units)]
        prep["g_k"] = [_dot_nt(prep["lhs"][i], bd(tile(prep["k_t"], u))) for i, u in enumerate(units)]

    def prep_products():
        g_b, g_k, bd_v = prep["g_b"], prep["g_k"], prep["bd_v"]
        idx = range(len(units))
        prep["rhs0"] = [_dot(jnp.where(strict, g_k[i][:CHUNK], 0.0), bd_v[i]) for i in idx]
        prep["okv"] = [_dot(jnp.where(incl, g_k[i][CHUNK:], 0.0), bd_v[i]) for i in idx]
        npow = [jnp.where(strict, -g_b[i][:CHUNK], 0.0) for i in idx]
        prep["tinv"] = [eye + npow[i] for i in idx]
        prep["npow"] = [_dot(npow[i], bd(npow[i])) for i in idx]

    def prep_double():
        npow, tinv = prep["npow"], prep["tinv"]
        both = [_dot(jnp.concatenate([npow[i], tinv[i]], axis=0), bd(npow[i])) for i in range(len(units))]
        prep["npow"] = [x[:CHUNK] for x in both]
        prep["tinv"] = [tinv[i] + both[i][CHUNK:] for i in range(len(units))]

    def prep_finish():
        npow, tinv = prep["npow"], prep["tinv"]
        prep["tinv"] = [tinv[i] + _dot(tinv[i], bd(npow[i])) for i in range(len(units))]

    state = [st_ref[p] for p in pairs]
    fin = {}

    def fin_from_state(j):
        c = RW_CB * (step - 1) + j
        seq, first, _ = _chunk_pos(jnp.maximum(c, 0), geom)
        first = jnp.logical_and(first, step >= 1)
        fin["seq"] = seq
        fin["st0"] = [jnp.where(first, si_ref[seq, p], state[p]) for p in pairs]
        fin["fs"] = [_dot_nt(lhs_s[rslot, j * n_pairs + p], fin["st0"][p]) for p in pairs]

    def fin_solve(j):
        fin["u"] = [_dot(tinv_s[rslot, j * n_pairs + p], bd(fin["fs"][p][:CHUNK] + rhs0_s[rslot, j * n_pairs + p]))
                    for p in pairs]

    def fin_out_state(j):
        fin["o"] = [fin["fs"][p][CHUNK:] + okv_s[rslot, j * n_pairs + p]
                    - _dot(brb_s[rslot, j * n_pairs + p], bd(fin["u"][p])) for p in pairs]
        for p in pairs:
            lhs_upd = jnp.concatenate([vb_s[rslot, rows[j], sls[p]], (-fin["u"][p]).astype(BF16)], axis=0)
            upd = _dot_tn(lhs_upd, keb_s[rslot, j * n_pairs + p])
            state[p] = fin["st0"][p] * decay_s[rslot, j, :, sls[p]] + jnp.where(bd_mask, upd, 0.0)
            so_ref[fin["seq"], p] = state[p]

    def fin_mean(j):
        mean = segsum(fin["o"])
        fin["oc"] = [fin["o"][p] - mean[p] * (1.0 / n) for p in pairs]

    def fin_write(j):
        var = segsum([x * x for x in fin["oc"]])
        for p in pairs:
            s = sls[p]
            on = fin["oc"][p] * lax.rsqrt(var[p] * (1.0 / n) + RW_GN_EPS) * lg_ref[:, s] + lb_ref[:, s]
            y_ref[rows[j], s] = (on + bonus_s[rslot, rows[j], s]) * g_ref[rows[j], s]

    prep_stages = [prep_norms, prep_gram, prep_products] + [prep_double] * 4 + [prep_finish]
    fin_stages = [functools.partial(f, j) for j in chunks
                  for f in (fin_from_state, fin_solve, fin_out_state, fin_mean, fin_write)]
    for k in range(max(len(prep_stages), len(fin_stages))):
        if k < len(fin_stages):
            fin_stages[k]()
        if k < len(prep_stages):
            prep_stages[k]()
    for p in pairs:
        st_ref[p] = state[p]

    for i, u in enumerate(units):
        lhs_s[wslot, i] = prep["lhs"][i]
        tinv_s[wslot, i] = prep["tinv"][i].astype(BF16)
        brb_s[wslot, i] = jnp.where(incl, prep["g_b"][i][CHUNK:], 0.0).astype(BF16)
        keb_s[wslot, i] = jnp.concatenate([tile(prep["k_e"], u), tile(prep["b_e"], u)], axis=0).astype(BF16)
        rhs0_s[wslot, i] = prep["rhs0"][i]
        okv_s[wslot, i] = prep["okv"][i]
        bonus_s[wslot, rows[u[0]], sls[u[1]]] = prep["bonus"][i] * tile(v, u)
    vb_s[wslot] = v.astype(BF16)
    for j in chunks:
        decay_s[wslot, j] = w_inc[rows[j]][CHUNK - 1:CHUNK]


def _rwkv(r, lw, kh, v, kkr, a, g, rk, lg, lb, s_in, geom: Geom):
    t, d = r.shape
    n = RW_N
    p2 = 2 * n
    n_pairs = d // p2
    n_seq = s_in.shape[0]
    assert geom.n_chunks % RW_CB == 0
    n_blocks = geom.n_chunks // RW_CB
    rb = RW_CB * CHUNK
    s4 = s_in.reshape(n_seq, n_pairs, 2, n, n)
    zero = jnp.zeros_like(s4[:, :, 0])
    s_bd = jnp.concatenate([jnp.concatenate([s4[:, :, 0], zero], axis=-1),
                            jnp.concatenate([zero, s4[:, :, 1]], axis=-1)], axis=-2)
    nxt = pl.BlockSpec((rb, d), lambda s: (jnp.minimum(s, n_blocks - 1), 0))
    cur = pl.BlockSpec((rb, d), lambda s: (jnp.maximum(s - 1, 0), 0))
    vec = _const_spec((1, d))
    per_unit = lambda rows, dtype: pltpu.VMEM((2, RW_CB * n_pairs, rows, p2), dtype)
    y, so = pl.pallas_call(
        functools.partial(_rwkv_kernel, geom=geom, n_pairs=n_pairs),
        grid=(n_blocks + 1,),
        in_specs=[nxt] * 6 + [cur] + [vec] * 3 + [_const_spec(s_bd.shape)],
        out_specs=[cur, _const_spec(s_bd.shape)],
        out_shape=[jax.ShapeDtypeStruct((t, d), F32), jax.ShapeDtypeStruct(s_bd.shape, F32)],
        scratch_shapes=[pltpu.VMEM((n_pairs, p2, p2), F32),
                        per_unit(p2, BF16), per_unit(CHUNK, BF16), per_unit(CHUNK, BF16), per_unit(p2, BF16),
                        pltpu.VMEM((2, rb, d), BF16), per_unit(CHUNK, F32), per_unit(CHUNK, F32),
                        pltpu.VMEM((2, rb, d), F32), pltpu.VMEM((2, RW_CB, 1, d), F32)],
        compiler_params=_params(),
        name="rwkv_scan",
    )(r, lw, kh, v, kkr, a, g, rk.reshape(1, d), lg.reshape(1, d), lb.reshape(1, d), s_bd)
    s_out = jnp.stack([so[:, :, :n, :n], so[:, :, n:, n:]], axis=2).reshape(s_in.shape)
    return y, s_out


def _rwkv_layer(x, gn, s_in, shift_in, p, j, geom):
    d = x[0].shape[1]
    seq_first = _first_chunks(geom)
    start_rows = jnp.zeros((geom.n_chunks, d), F32).at[seq_first].set(shift_in)
    r, lw, kh, v, kkr, a, g, hl = _rwkv_proj(x, gn, start_rows, p, j, geom)
    y, s_out = _rwkv(r, lw, kh, v, kkr, a, g, p["rw_rk"][j], p["rw_ln_g"][j], p["rw_ln_b"][j], s_in, geom)
    return y, s_out, hl.reshape(geom.n_chunks, d)[_last_chunks(geom)]


def _first_chunks(g: Geom):
    return np.concatenate([np.arange(g.nb_p) * g.cps_p,
                           g.npc + np.arange(g.n_seq - g.nb_p) * g.cps_s])


def _last_chunks(g: Geom):
    return _first_chunks(g) + np.concatenate([np.full(g.nb_p, g.cps_p - 1),
                                              np.full(g.n_seq - g.nb_p, g.cps_s - 1)])


def _gelu_tanh(x):
    return 0.5 * x * (1.0 + jnp.tanh(0.7978845608028654 * (x + 0.044715 * x * x * x)))


def _lru_proj_kernel(*refs, n_x, np_tiles):
    x_refs, (gn_ref, wy_ref, wx_ref, y_ref, ux_ref) = refs[:n_x], refs[n_x:]
    h = _rms(_slab_tile(x_refs, np_tiles), gn_ref[...]).astype(BF16)
    y_ref[...] = _gelu_tanh(jnp.dot(h, wy_ref[...], preferred_element_type=F32))
    ux_ref[...] = jnp.dot(h, wx_ref[...], preferred_element_type=F32)


def _lru_proj(x, gn, wy, wx, *, tm=512):
    d = x[0].shape[1]
    rows, np_tiles, n_tiles = _slab_specs([a.shape for a in x], tm)
    out = jax.ShapeDtypeStruct((n_tiles * tm, wy.shape[1]), F32)
    return pl.pallas_call(
        functools.partial(_lru_proj_kernel, n_x=len(x), np_tiles=np_tiles),
        grid=(n_tiles,),
        in_specs=rows + [_const_spec((1, d)), _const_spec(wy.shape), _const_spec(wx.shape)],
        out_specs=[pl.BlockSpec((tm, wy.shape[1]), lambda i: (i, 0))] * 2,
        out_shape=[out, out],
        compiler_params=_params("parallel"),
        name="lru_proj",
    )(*x, gn.reshape(1, d), wy.astype(BF16), wx.astype(BF16))


def _shift_rows(x, d, fill, row):
    return jnp.where(row < d, fill, pltpu.roll(x, d, 0))


def _lru_kernel(ux_ref, y_ref, cw_ref, cb_ref, gaw_ref, gab_ref, gxw_ref, gxb_ref, lam_ref,
                hi_ref, bi_ref, hy_ref, ho_ref, bo_ref, win_ref, carry_ref, hc_ref, *, geom, conv_k):
    step = pl.program_id(0)
    top = SUBLANES
    tail = conv_k - 1
    chunks = range(CONV_CB)
    rows = [slice(j * CHUNK, (j + 1) * CHUNK) for j in chunks]

    @pl.when(step == 0)
    def _():
        carry_ref[...] = jnp.zeros(carry_ref.shape, F32)
        hc_ref[...] = jnp.zeros(hc_ref.shape, F32)

    pos = [_chunk_pos(step * CONV_CB + j, geom) for j in chunks]
    for j in chunks:
        seq, first, _ = pos[j]
        before = carry_ref[...] if j == 0 else ux_ref[j * CHUNK - top:j * CHUNK]
        win_ref[j, :top] = jnp.where(first, bi_ref[seq], before)
        win_ref[j, top:] = ux_ref[rows[j]]
        _window_conv(win_ref.at[j], cw_ref, cb_ref, hy_ref, top - tail, conv_k, out_row=j * CHUNK)
        bo_ref[seq] = ux_ref[(j + 1) * CHUNK - top:(j + 1) * CHUNK]
    carry_ref[...] = ux_ref[CONV_CB * CHUNK - top:]
    u = hy_ref[...]

    bw = u.shape[1] // LRU_BLOCKS
    zr, zi = [], []
    for blk in range(LRU_BLOCKS):
        ub = u[:, blk * bw:(blk + 1) * bw].astype(BF16)
        zr.append(jnp.dot(ub, gaw_ref[blk], preferred_element_type=F32))
        zi.append(jnp.dot(ub, gxw_ref[blk], preferred_element_type=F32))
    r = _sigmoid(jnp.concatenate(zr, axis=1) + gab_ref[...])
    i = _sigmoid(jnp.concatenate(zi, axis=1) + gxb_ref[...])
    nl = -lam_ref[...]
    softplus = jnp.maximum(nl, 0.0) + jnp.log1p(jnp.exp(-jnp.abs(nl)))
    log_a = -LRU_C * r * softplus
    a = jnp.exp(log_a)
    bterm = jnp.sqrt(-jnp.tanh(log_a) * (a * a + 1.0)) * (i * u)

    row = lax.broadcasted_iota(jnp.int32, a.shape, 0) % CHUNK
    d = 1
    while d < CHUNK:
        a_s = _shift_rows(a, d, 1.0, row)
        b_s = _shift_rows(bterm, d, 0.0, row)
        bterm = bterm + a * b_s
        a = a * a_s
        d *= 2
    h_prev = hc_ref[...]
    for j in chunks:
        seq, first, _ = pos[j]
        h = a[rows[j]] * jnp.where(first, hi_ref[seq], h_prev) + bterm[rows[j]]
        hy_ref[rows[j]] = h * y_ref[rows[j]]
        h_prev = h[CHUNK - 1:CHUNK]
        ho_ref[seq] = h_prev
    hc_ref[...] = h_prev


def _lru(ux, y, h_in, buf_in, p, j, geom: Geom):
    t, w = ux.shape
    conv_k = p["lru_conv_w"].shape[1]
    tail = conv_k - 1
    assert geom.n_chunks % CONV_CB == 0 and tail <= SUBLANES
    row = pl.BlockSpec((CONV_CB * CHUNK, w), lambda c: (c, 0))
    vec = _const_spec((1, w))
    gspec = _const_spec(p["lru_ga_w"].shape[1:])
    h3 = h_in.reshape(geom.n_seq, 1, w)
    b8 = jnp.pad(buf_in, ((0, 0), (SUBLANES - tail, 0), (0, 0)))
    hy, ho, bo = pl.pallas_call(
        functools.partial(_lru_kernel, geom=geom, conv_k=conv_k),
        grid=(geom.n_chunks // CONV_CB,),
        in_specs=[row, row, _const_spec((conv_k, w)), vec, gspec, vec, gspec, vec, vec,
                  _const_spec(h3.shape), _const_spec(b8.shape)],
        out_specs=[row, _const_spec(h3.shape), _const_spec(b8.shape)],
        out_shape=[jax.ShapeDtypeStruct((t, w), F32), jax.ShapeDtypeStruct(h3.shape, F32),
                   jax.ShapeDtypeStruct(b8.shape, F32)],
        scratch_shapes=[pltpu.VMEM((CONV_CB, CHUNK + SUBLANES, w), F32), pltpu.VMEM((SUBLANES, w), F32),
                        pltpu.VMEM((1, w), F32)],
        compiler_params=_params(),
        name="lru_scan",
    )(ux, y, p["lru_conv_w"][j], p["lru_conv_b"][j].reshape(1, w), p["lru_ga_w"][j].astype(BF16),
      p["lru_ga_b"][j].reshape(1, w), p["lru_gx_w"][j].astype(BF16), p["lru_gx_b"][j].reshape(1, w),
      p["lru_lam"][j].reshape(1, w), h3, b8)
    return hy, ho, bo[:, SUBLANES - tail:]


def _lru_layer(x, gn, h_in, buf_in, p, j, geom):
    y, ux = _lru_proj(x, gn, p["lru_wy"][j], p["lru_wx"][j])
    hy, h_out, buf_out = _lru(ux, y, h_in, buf_in, p, j, geom)
    return hy, h_out.reshape(h_in.shape), buf_out


def _conf_proj_kernel(*refs, n_x, np_tiles):
    x_refs, (gn_ref, w1_ref, b1_ref, u_ref) = refs[:n_x], refs[n_x:]
    d = u_ref.shape[1]
    h = _rms(_slab_tile(x_refs, np_tiles), gn_ref[...]).astype(BF16)
    hh = jnp.dot(h, w1_ref[...], preferred_element_type=F32) + b1_ref[...]
    u_ref[...] = hh[:, :d] * _sigmoid(hh[:, d:])


def _conf_proj(x, gn, w1, b1, *, tm=512):
    d = x[0].shape[1]
    rows, np_tiles, n_tiles = _slab_specs([a.shape for a in x], tm)
    return pl.pallas_call(
        functools.partial(_conf_proj_kernel, n_x=len(x), np_tiles=np_tiles),
        grid=(n_tiles,),
        in_specs=rows + [_const_spec((1, d)), _const_spec(w1.shape), _const_spec((1, w1.shape[1]))],
        out_specs=pl.BlockSpec((tm, d), lambda i: (i, 0)),
        out_shape=jax.ShapeDtypeStruct((n_tiles * tm, d), F32),
        compiler_params=_params("parallel"),
        name="conf_proj",
    )(*x, gn.reshape(1, d), w1.astype(BF16), b1.reshape(1, -1))


def _window_conv(win_ref, w_ref, b_ref, out_ref, base, conv_k, out_row=0):
    for lt in range(out_ref.shape[1] // LANES):
        sl = slice(lt * LANES, (lt + 1) * LANES)
        acc = jnp.broadcast_to(b_ref[:, sl], (CHUNK, LANES))
        for b in range(SUBLANES):
            offs = [o for o in range(base, base + conv_k) if o % SUBLANES == b]
            if not offs:
                continue
            rows = CHUNK + SUBLANES if b else CHUNK
            part = sum(w_ref[o - base:o - base + 1, sl] * win_ref[o - b:o - b + rows, sl] for o in offs)
            acc = acc + (pltpu.roll(part, rows - b, 0)[:CHUNK] if b else part)
        out_ref[out_row:out_row + CHUNK, sl] = acc


def _conf_kernel(u_ref, w_ref, b_ref, lg_ref, lb_ref, bi_ref, c_ref, bo_ref, win_ref, carry_ref,
                 *, geom, conv_k, top):
    step = pl.program_id(0)
    tail = conv_k - 1

    @pl.when(step == 0)
    def _():
        carry_ref[...] = jnp.zeros(carry_ref.shape, F32)

    for j in range(CONV_CB):
        seq, first, _ = _chunk_pos(step * CONV_CB + j, geom)
        before = carry_ref[...] if j == 0 else u_ref[j * CHUNK - top:j * CHUNK]
        win_ref[j, :top] = jnp.where(first, bi_ref[seq], before)
        win_ref[j, top:] = u_ref[j * CHUNK:(j + 1) * CHUNK]
        _window_conv(win_ref.at[j], w_ref, b_ref, c_ref, top - tail, conv_k, out_row=j * CHUNK)
        bo_ref[seq] = u_ref[(j + 1) * CHUNK - top:(j + 1) * CHUNK]
    carry_ref[...] = u_ref[CONV_CB * CHUNK - top:]
    acc = c_ref[...]
    mu = jnp.mean(acc, axis=-1, keepdims=True)
    xc = acc - mu
    var = jnp.mean(xc * xc, axis=-1, keepdims=True)
    c_ref[...] = _silu(xc * lax.rsqrt(var + EPS) * lg_ref[...] + lb_ref[...])


def _conf(u, buf_in, p, j, geom: Geom):
    t, d = u.shape
    conv_k = p["cf_dw_w"].shape[1]
    tail = conv_k - 1
    top = -(-tail // SUBLANES) * SUBLANES
    assert top <= CHUNK and geom.n_chunks % CONV_CB == 0
    row = pl.BlockSpec((CONV_CB * CHUNK, d), lambda c: (c, 0))
    vec = _const_spec((1, d))
    b_top = jnp.pad(buf_in, ((0, 0), (top - tail, 0), (0, 0)))
    c, bo = pl.pallas_call(
        functools.partial(_conf_kernel, geom=geom, conv_k=conv_k, top=top),
        grid=(geom.n_chunks // CONV_CB,),
        in_specs=[row, _const_spec((conv_k, d)), vec, vec, vec, _const_spec(b_top.shape)],
        out_specs=[row, _const_spec(b_top.shape)],
        out_shape=[jax.ShapeDtypeStruct((t, d), F32), jax.ShapeDtypeStruct(b_top.shape, F32)],
        scratch_shapes=[pltpu.VMEM((CONV_CB, CHUNK + top, d), F32), pltpu.VMEM((top, d), F32)],
        compiler_params=_params(),
        name="conf_conv",
    )(u, p["cf_dw_w"][j], p["cf_dw_b"][j].reshape(1, d), p["cf_ln_g"][j].reshape(1, d),
      p["cf_ln_b"][j].reshape(1, d), b_top)
    return c, bo[:, top - tail:]


def _conf_layer(x, gn, buf_in, p, j, geom):
    u = _conf_proj(x, gn, p["cf_w1"][j], p["cf_b1"][j])
    return _conf(u, buf_in, p, j, geom)


_ARG_NAMES = (
    "x_prompt x_sample state_hgrn state_rwkv state_rwkv_shift state_lru state_lru_conv state_conf_conv "
    "norm_mix norm_ffn norm_final hg_wq hg_wf hg_wi hg_wg hg_gn hg_wo hg_lb rw_mu rw_wr rw_wk rw_wv rw_w0 "
    "rw_w1 rw_w2 rw_a0 rw_a1 rw_a2 rw_g1 rw_g2 rw_kk rw_ka rw_rk rw_ln_g rw_ln_b rw_wo lru_wy lru_wx "
    "lru_conv_w lru_conv_b lru_ga_w lru_ga_b lru_gx_w lru_gx_b lru_lam lru_wo cf_w1 cf_b1 cf_dw_w cf_dw_b "
    "cf_ln_g cf_ln_b cf_w2 cf_b2 ffn_w1 ffn_w3 ffn_w2").split()
N_MIXERS = 4


def kernel(x_prompt, x_sample, state_hgrn, state_rwkv, state_rwkv_shift, state_lru, state_lru_conv, state_conf_conv, norm_mix, norm_ffn, norm_final, hg_wq, hg_wf, hg_wi, hg_wg, hg_gn, hg_wo, hg_lb, rw_mu, rw_wr, rw_wk, rw_wv, rw_w0, rw_w1, rw_w2, rw_a0, rw_a1, rw_a2, rw_g1, rw_g2, rw_kk, rw_ka, rw_rk, rw_ln_g, rw_ln_b, rw_wo, lru_wy, lru_wx, lru_conv_w, lru_conv_b, lru_ga_w, lru_ga_b, lru_gx_w, lru_gx_b, lru_lam, lru_wo, cf_w1, cf_b1, cf_dw_w, cf_dw_b, cf_ln_g, cf_ln_b, cf_w2, cf_b2, ffn_w1, ffn_w3, ffn_w2):
    p = dict(zip(_ARG_NAMES, (x_prompt, x_sample, state_hgrn, state_rwkv, state_rwkv_shift, state_lru, state_lru_conv, state_conf_conv, norm_mix, norm_ffn, norm_final, hg_wq, hg_wf, hg_wi, hg_wg, hg_gn, hg_wo, hg_lb, rw_mu, rw_wr, rw_wk, rw_wv, rw_w0, rw_w1, rw_w2, rw_a0, rw_a1, rw_a2, rw_g1, rw_g2, rw_kk, rw_ka, rw_rk, rw_ln_g, rw_ln_b, rw_wo, lru_wy, lru_wx, lru_conv_w, lru_conv_b, lru_ga_w, lru_ga_b, lru_gx_w, lru_gx_b, lru_lam, lru_wo, cf_w1, cf_b1, cf_dw_w, cf_dw_b, cf_ln_g, cf_ln_b, cf_w2, cf_b2, ffn_w1, ffn_w3, ffn_w2)))
    nb_p, seq_p, d = x_prompt.shape
    nb_s, seq_s, _ = x_sample.shape
    assert seq_p % CHUNK == 0 and seq_s % CHUNK == 0
    npc = nb_p * seq_p // CHUNK
    geom = Geom(n_chunks=npc + nb_s * seq_s // CHUNK, npc=npc, cps_p=seq_p // CHUNK,
                cps_s=seq_s // CHUNK, nb_p=nb_p, n_seq=nb_p + nb_s)
    x = (x_prompt.reshape(-1, d), x_sample.reshape(-1, d))
    split_shapes = [a.shape for a in x]
    n_rows = sum(s[0] for s in split_shapes)

    def all_seqs(state):
        return jnp.concatenate([jnp.zeros((nb_p,) + state.shape[1:], state.dtype), state], axis=0)

    depth = norm_mix.shape[0]
    zero_bias = jnp.zeros((d,), F32)
    outs = {k: [] for k in ("hg", "rw", "sh", "lh", "lc", "cf")}
    for i in range(depth):
        m, j = i % N_MIXERS, i // N_MIXERS
        gn = norm_mix[i]
        bo = zero_bias
        if m == 0:
            y, s = _hgrn_layer(x, gn, all_seqs(state_hgrn[j]), p, j, geom)
            outs["hg"].append(s)
            wo = hg_wo[j]
        elif m == 1:
            y, s, sh = _rwkv_layer(x, gn, all_seqs(state_rwkv[j]), all_seqs(state_rwkv_shift[j]), p, j, geom)
            outs["rw"].append(s)
            outs["sh"].append(sh)
            wo = rw_wo[j]
        elif m == 2:
            y, hl, cb = _lru_layer(x, gn, all_seqs(state_lru[j]), all_seqs(state_lru_conv[j]), p, j, geom)
            outs["lh"].append(hl)
            outs["lc"].append(cb)
            wo = lru_wo[j]
        else:
            y, cb = _conf_layer(x, gn, all_seqs(state_conf_conv[j]), p, j, geom)
            outs["cf"].append(cb)
            wo, bo = cf_w2[j], cf_b2[j]
        final = i == depth - 1
        out_shapes = split_shapes if final else [(n_rows, d)]
        x = _post_ffn(x, y, wo, bo, norm_ffn[i], ffn_w1[i], ffn_w3[i], ffn_w2[i], norm_final,
                      final=final, out_shapes=out_shapes)

    stacked = [jnp.stack(outs[k]) for k in ("hg", "rw", "sh", "lh", "lc", "cf")]
    return ((x[0].reshape(nb_p, seq_p, d), x[1].reshape(nb_s, seq_s, d))
            + tuple(s[:, :nb_p] for s in stacked) + tuple(s[:, nb_p:] for s in stacked))
```
